```python
import math
import jax, jax.numpy as jnp
from jax import lax
import numpy as np

D_MODEL = 1024
BATCH = 4
SEQ = 4096
DEPTH = 4
DEC_BATCH = 128
DEC_SEQ = 4
PAST_LEN = 2048
PAGE_SIZE = 128

D_MIX = D_MODEL
HEAD_DIM = 64
D_A = 3 * D_MIX // 8
D_B = D_MIX // 4
D_C = D_MIX - D_A - D_B
N_A_HEADS = D_A // HEAD_DIM
N_C_HEADS = D_C // HEAD_DIM
POOL_WINDOWS = (2, 4, 8, 16)
N_POOL_GROUPS = len(POOL_WINDOWS)
POOL_GROUP = D_B // N_POOL_GROUPS
POOL_BUF = max(POOL_WINDOWS) - 1
CONV_WIDTH = 4
RG_C = 8.0
D_IN = 2 * D_A + D_B + 3 * D_C
D_FF = ((8 * D_MODEL // 3 + 255) // 256) * 256
D_PLE = 256
Q_BLOCK = 128
SB_BIAS_INIT = -6.0
EPS = 1e-6

kernel_name = 'hybrid_rglru_pool_stickbreak_decoder_step'


def rmsnorm(x, g):
    xf = x.astype(jnp.float32)
    y = xf * lax.rsqrt(jnp.mean(xf * xf, axis=-1, keepdims=True) + EPS)
    return (y * g.astype(jnp.float32)).astype(x.dtype)


def causal_conv(x, buf, w, b):
    T = x.shape[1]
    xe = jnp.concatenate([buf.astype(x.dtype), x], axis=1)
    y = b + sum(w[j] * xe[:, j:j + T] for j in range(CONV_WIDTH))
    return y, xe[:, -(CONV_WIDTH - 1):]


def rglru(x, h0, wa, ba, wx, bx, lam):
    N, T, _ = x.shape
    xh = x.reshape(N, T, N_A_HEADS, HEAD_DIM)
    r = jax.nn.sigmoid(jnp.einsum('bthi,hij->bthj', xh, wa).reshape(N, T, D_A) + ba)
    i = jax.nn.sigmoid(jnp.einsum('bthi,hij->bthj', xh, wx).reshape(N, T, D_A) + bx)
    log_a = RG_C * r.astype(jnp.float32) * jax.nn.log_sigmoid(lam.astype(jnp.float32))
    a = jnp.exp(log_a)
    u = jnp.sqrt(-jnp.expm1(2.0 * log_a)) * (i * x).astype(jnp.float32)

    def step(h, au):
        a_t, u_t = au
        h = a_t * h + u_t
        return h, h

    h_last, hs = lax.scan(step, h0.astype(jnp.float32), (jnp.swapaxes(a, 0, 1), jnp.swapaxes(u, 0, 1)))
    return jnp.swapaxes(hs, 0, 1).astype(x.dtype), h_last.astype(x.dtype)


def pool_mix(xb, buf, start, w, scale):
    N, T, _ = xb.shape
    xe = jnp.concatenate([buf.astype(xb.dtype), xb], axis=1)
    cs = jnp.cumsum(xe.astype(jnp.float32), axis=1)
    cs = jnp.concatenate([jnp.zeros_like(cs[:, :1]), cs], axis=1)
    pos = start + jnp.arange(T)
    hi = POOL_BUF + 1
    means = []
    for g, win in enumerate(POOL_WINDOWS):
        sl = slice(g * POOL_GROUP, (g + 1) * POOL_GROUP)
        wsum = cs[:, hi:hi + T, sl] - cs[:, hi - win:hi - win + T, sl]
        cnt = jnp.minimum(win, pos + 1).astype(jnp.float32)[None, :, None]
        means.append(wsum / cnt)
    pooled = jnp.concatenate(means, axis=-1) - xb.astype(jnp.float32)
    pooled = pooled.astype(xb.dtype).reshape(N, T, N_POOL_GROUPS, POOL_GROUP)
    y = jnp.einsum('btgi,gij->btgj', pooled, w).reshape(N, T, D_B) * scale
    return y, xe[:, -POOL_BUF:]


def stick_core(q, k, v, q_pos, k_pos, bias):
    z = jnp.einsum('bqhd,bkhd->bhqk', q, k).astype(jnp.float32) * (HEAD_DIM ** -0.5)
    z = z + bias.astype(jnp.float32)[None, :, None, None]
    mask = k_pos[None, :] < q_pos[:, None]
    log_1mb = jnp.where(mask, jax.nn.log_sigmoid(-z), 0.0)
    suffix = lax.cumsum(log_1mb, axis=3, reverse=True) - log_1mb
    A = jnp.where(mask, jnp.exp(jax.nn.log_sigmoid(z) + suffix), 0.0)
    return jnp.einsum('bhqk,bkhd->bqhd', A.astype(v.dtype), v)


def stick_prompt(q, k, v, bias):
    T = q.shape[1]
    outs = []
    for s0 in range(0, T, Q_BLOCK):
        e = min(s0 + Q_BLOCK, T)
        outs.append(stick_core(q[:, s0:e], k[:, :e], v[:, :e], jnp.arange(s0, e), jnp.arange(e), bias))
    return jnp.concatenate(outs, axis=1)


def stick_decode(q, k, v, k_past, v_past, bias):
    T = q.shape[1]
    past = k_past.shape[1]
    k_all = jnp.concatenate([k_past.astype(k.dtype), k], axis=1)
    v_all = jnp.concatenate([v_past.astype(v.dtype), v], axis=1)
    return stick_core(q, k_all, v_all, past + jnp.arange(T), jnp.arange(past + T), bias)


def trunk_layer(h, pe, lw, conv_buf, rg_h0, pool_buf, k_past, v_past, start):
    N, T, _ = h.shape
    hn = rmsnorm(h, lw['norm_mix'])
    u = hn @ lw['w_in']
    s1 = D_A; s2 = 2 * D_A; s3 = s2 + D_B; s4 = s3 + D_C; s5 = s4 + D_C
    xa, ga, xb, q, k, v = jnp.split(u, [s1, s2, s3, s4, s5], axis=-1)
    xc, conv_new = causal_conv(xa, conv_buf, lw['conv_w'], lw['conv_b'])
    hs, rg_new = rglru(xc, rg_h0, lw['rg_wa'], lw['rg_ba'], lw['rg_wx'], lw['rg_bx'], lw['rg_lambda'])
    y_a = rmsnorm(jax.nn.gelu(ga) * hs, lw['norm_a'])
    y_b, pool_new = pool_mix(xb, pool_buf, start, lw['pool_w'], lw['pool_scale'])
    q = q.reshape(N, T, N_C_HEADS, HEAD_DIM)
    k = k.reshape(N, T, N_C_HEADS, HEAD_DIM)
    v = v.reshape(N, T, N_C_HEADS, HEAD_DIM)
    if k_past is None:
        o = stick_prompt(q, k, v, lw['sb_bias'])
    else:
        o = stick_decode(q, k, v, k_past, v_past, lw['sb_bias'])
    y_c = rmsnorm(o.reshape(N, T, D_C), lw['norm_c'])
    h = h + jnp.concatenate([y_a, y_b, y_c], axis=-1) @ lw['w_out']
    hn = rmsnorm(h, lw['norm_ffn'])
    h = h + (jax.nn.silu(hn @ lw['w_ff_gate']) * (hn @ lw['w_ff_up'])) @ lw['w_ff_down']
    gate = jax.nn.sigmoid(rmsnorm(h, lw['norm_ple']) @ lw['w_ple_gate'])
    h = h + gate * (pe @ lw['w_ple_proj'])
    return h, k, v, conv_new, rg_new, pool_new


def setup_inputs(seed: int = 0) -> dict:
    key = jax.random.key(seed)
    ks = jax.random.split(key, 40)
    f32 = jnp.float32
    n_pages = PAST_LEN // PAGE_SIZE
    n_used = DEC_BATCH * n_pages
    n_pool = n_used + n_used // 4

    def nrm(k, shape, scale):
        return scale * jax.random.normal(k, shape, f32)

    def gain(k, shape):
        return 1.0 + 0.02 * jax.random.normal(k, shape, f32)

    a0 = jax.random.uniform(ks[20], (DEPTH, D_A), f32, 0.9, 0.999) ** (1.0 / RG_C)
    rg_lambda = jnp.log(a0) - jnp.log1p(-a0)
    page_table = jax.random.permutation(ks[7], n_pool)[:n_used].reshape(DEC_BATCH, n_pages).astype(jnp.int32)
    return {
        'x_prompt': nrm(ks[0], (BATCH, SEQ, D_MODEL), 1.0),
        'x_sample': nrm(ks[1], (DEC_BATCH, DEC_SEQ, D_MODEL), 1.0),
        'cache_k': nrm(ks[2], (DEPTH, n_pool, PAGE_SIZE, N_C_HEADS, HEAD_DIM), 1.0),
        'cache_v': nrm(ks[3], (DEPTH, n_pool, PAGE_SIZE, N_C_HEADS, HEAD_DIM), 1.0),
        'state_conv': nrm(ks[4], (DEPTH, DEC_BATCH, CONV_WIDTH - 1, D_A), 1.0),
        'state_rglru': nrm(ks[5], (DEPTH, DEC_BATCH, D_A), 0.5),
        'state_pool': nrm(ks[6], (DEPTH, DEC_BATCH, POOL_BUF, D_B), 1.0),
        'page_table': page_table,
        'p_prompt': nrm(ks[8], (DEPTH, BATCH, SEQ, D_PLE), 1.0),
        'p_sample': nrm(ks[9], (DEPTH, DEC_BATCH, DEC_SEQ, D_PLE), 1.0),
        'norm_mix': gain(ks[10], (DEPTH, D_MODEL)),
        'w_in': nrm(ks[11], (DEPTH, D_MODEL, D_IN), D_MODEL ** -0.5),
        'conv_w': nrm(ks[12], (DEPTH, CONV_WIDTH, D_A), CONV_WIDTH ** -0.5),
        'conv_b': nrm(ks[13], (DEPTH, D_A), 0.01),
        'rg_wa': nrm(ks[14], (DEPTH, N_A_HEADS, HEAD_DIM, HEAD_DIM), HEAD_DIM ** -0.5),
        'rg_ba': nrm(ks[15], (DEPTH, D_A), 0.01),
        'rg_wx': nrm(ks[16], (DEPTH, N_A_HEADS, HEAD_DIM, HEAD_DIM), HEAD_DIM ** -0.5),
        'rg_bx': nrm(ks[17], (DEPTH, D_A), 0.01),
        'rg_lambda': rg_lambda,
        'norm_a': gain(ks[18], (DEPTH, D_A)),
        'pool_w': nrm(ks[19], (DEPTH, N_POOL_GROUPS, POOL_GROUP, POOL_GROUP), POOL_GROUP ** -0.5),
        'pool_scale': gain(ks[21], (DEPTH, D_B)),
        'norm_c': gain(ks[22], (DEPTH, D_C)),
        'sb_bias': SB_BIAS_INIT + 0.1 * jax.random.normal(ks[32], (DEPTH, N_C_HEADS), f32),
        'w_out': nrm(ks[23], (DEPTH, D_MIX, D_MODEL), D_MIX ** -0.5),
        'norm_ffn': gain(ks[24], (DEPTH, D_MODEL)),
        'w_ff_gate': nrm(ks[25], (DEPTH, D_MODEL, D_FF), D_MODEL ** -0.5),
        'w_ff_up': nrm(ks[26], (DEPTH, D_MODEL, D_FF), D_MODEL ** -0.5),
        'w_ff_down': nrm(ks[27], (DEPTH, D_FF, D_MODEL), D_FF ** -0.5),
        'norm_ple': gain(ks[28], (DEPTH, D_MODEL)),
        'w_ple_gate': nrm(ks[29], (DEPTH, D_MODEL, D_MODEL), D_MODEL ** -0.5),
        'w_ple_proj': nrm(ks[30], (DEPTH, D_PLE, D_MODEL), D_PLE ** -0.5),
        'norm_final': gain(ks[31], (D_MODEL,)),
    }


def reference(x_prompt, x_sample, cache_k, cache_v, state_conv, state_rglru, state_pool, page_table,
              p_prompt, p_sample, norm_mix, w_in, conv_w, conv_b, rg_wa, rg_ba, rg_wx, rg_bx, rg_lambda,
              norm_a, pool_w, pool_scale, norm_c, sb_bias, w_out, norm_ffn, w_ff_gate, w_ff_up, w_ff_down,
              norm_ple, w_ple_gate, w_ple_proj, norm_final):
    nb = x_prompt.shape[0]
    nd = x_sample.shape[0]
    dt = x_prompt.dtype
    hp = x_prompt
    hs = x_sample
    kp, vp, ksm, vsm = [], [], [], []
    cp, csm, rp, rsm, pp, psm = [], [], [], [], [], []
    for l in range(DEPTH):
        lw = {
            'norm_mix': norm_mix[l], 'w_in': w_in[l], 'conv_w': conv_w[l], 'conv_b': conv_b[l],
            'rg_wa': rg_wa[l], 'rg_ba': rg_ba[l], 'rg_wx': rg_wx[l], 'rg_bx': rg_bx[l],
            'rg_lambda': rg_lambda[l], 'norm_a': norm_a[l], 'pool_w': pool_w[l],
            'pool_scale': pool_scale[l], 'norm_c': norm_c[l], 'sb_bias': sb_bias[l], 'w_out': w_out[l],
            'norm_ffn': norm_ffn[l], 'w_ff_gate': w_ff_gate[l], 'w_ff_up': w_ff_up[l],
            'w_ff_down': w_ff_down[l], 'norm_ple': norm_ple[l], 'w_ple_gate': w_ple_gate[l],
            'w_ple_proj': w_ple_proj[l],
        }
        hp, k1, v1, c1, r1, o1 = trunk_layer(
            hp, p_prompt[l], lw,
            jnp.zeros((nb, CONV_WIDTH - 1, D_A), dt), jnp.zeros((nb, D_A), dt),
            jnp.zeros((nb, POOL_BUF, D_B), dt), None, None, 0)
        k_past = cache_k[l][page_table].reshape(nd, -1, N_C_HEADS, HEAD_DIM)
        v_past = cache_v[l][page_table].reshape(nd, -1, N_C_HEADS, HEAD_DIM)
        hs, k2, v2, c2, r2, o2 = trunk_layer(
            hs, p_sample[l], lw, state_conv[l], state_rglru[l], state_pool[l],
            k_past, v_past, k_past.shape[1])
        kp.append(k1); vp.append(v1); ksm.append(k2); vsm.append(v2)
        cp.append(c1); csm.append(c2); rp.append(r1); rsm.append(r2); pp.append(o1); psm.append(o2)
    y_prompt = rmsnorm(hp, norm_final)
    y_sample = rmsnorm(hs, norm_final)
    return (y_prompt, y_sample, jnp.stack(kp), jnp.stack(vp), jnp.stack(ksm), jnp.stack(vsm),
            jnp.stack(cp), jnp.stack(csm), jnp.stack(rp), jnp.stack(rsm), jnp.stack(pp), jnp.stack(psm))
```

```python
import functools

import jax
import jax.numpy as jnp
from jax import lax
from jax.experimental import pallas as pl
from jax.experimental.pallas import tpu as pltpu

F32 = jnp.float32
BF16 = jnp.bfloat16

D_MODEL = 1024
BATCH = 4
SEQ = 4096
DEPTH = 4
DEC_BATCH = 128
DEC_SEQ = 4
PAST_LEN = 2048
PAGE_SIZE = 128
N_PAGES = PAST_LEN // PAGE_SIZE
HEAD_DIM = 64
D_A = 384
D_B = 256
D_C = 384
N_C_HEADS = 6
POOL_WINDOWS = (2, 4, 8, 16)
POOL_GROUP = 64
POOL_BUF = 15
CONV_WIDTH = 4
RG_C = 8.0
D_IN = 2 * D_A + D_B + 3 * D_C
D_FF = 2816
D_PLE = 256
EPS = 1e-6
SCALE = HEAD_DIM ** -0.5

NP = BATCH * SEQ
NS = DEC_BATCH * DEC_SEQ
NT = NP + NS
TM = 512
NPT = NP // TM
NTT = NT // TM
TT = 512
TQ = 256
TK = 256
LANE = 128
HEAD_PAIR = 2 * HEAD_DIM
N_PAIRS = D_C // HEAD_PAIR
VMEM_LIMIT = 56 * 1024 * 1024


def _rms(x, g):
    ms = jnp.mean(x * x, axis=-1, keepdims=True)
    return x * lax.rsqrt(ms + EPS) * g


def _dot(a, b):
    return jnp.dot(a, b, preferred_element_type=F32)


def _dot_nt(a, b):
    return lax.dot_general(a, b, (((1,), (1,)), ((), ())), preferred_element_type=F32)


def _neg_softplus(z):
    return -(jnp.maximum(z, 0.0) + jnp.log1p(jnp.exp(-jnp.abs(z))))


def _split3(x):
    hi = x.astype(BF16)
    r = x - hi.astype(F32)
    mid = r.astype(BF16)
    lo = (r - mid.astype(F32)).astype(BF16)
    return hi, mid, lo


def _suffix_sums(l, upper):
    hi, mid, lo = _split3(l)
    return _dot(hi, upper) + _dot(mid, upper) + _dot(lo, upper)


def _strict_lower_ones(n):
    r = lax.broadcasted_iota(jnp.int32, (n, n), 0)
    c = lax.broadcasted_iota(jnp.int32, (n, n), 1)
    return jnp.where(r > c, 1.0, 0.0).astype(BF16)


def _layer_spec(shape, layer):
    nd = len(shape)
    return pl.BlockSpec((None,) + tuple(shape), lambda *_: (layer,) + (0,) * nd,
                        pipeline_mode=pl.Buffered(1))


def _params(*sem):
    return pltpu.CompilerParams(dimension_semantics=sem, vmem_limit_bytes=VMEM_LIMIT)


def _in_proj_kernel(h_ref, g_ref, w_ref, xag_ref, xb_ref, k_ref, v_ref, qb_ref, kb_ref, vb_ref):
    hn = _rms(h_ref[...], g_ref[...]).astype(BF16)
    u = _dot(hn, w_ref[...])
    s2 = 2 * D_A
    s3 = s2 + D_B
    s4 = s3 + D_C
    s5 = s4 + D_C
    xag_ref[...] = u[:, :s2]
    xb_ref[...] = u[:, s2:s3]
    qb_ref[...] = u[:, s3:s4].astype(BF16)
    k = u[:, s4:s5]
    v = u[:, s5:]
    k_ref[...] = k
    v_ref[...] = v
    kb_ref[...] = k.astype(BF16)
    vb_ref[...] = v.astype(BF16)


def _in_proj(h, norm_mix, w_in_b, layer):
    row = lambda w: pl.BlockSpec((TM, w), lambda i: (i, 0))
    return pl.pallas_call(
        _in_proj_kernel,
        grid=(NTT,),
        in_specs=[row(D_MODEL), _layer_spec((1, D_MODEL), layer), _layer_spec((D_MODEL, D_IN), layer)],
        out_specs=[row(2 * D_A), row(D_B), row(D_C), row(D_C), row(D_C), row(D_C), row(D_C)],
        out_shape=[jax.ShapeDtypeStruct((NT, 2 * D_A), F32), jax.ShapeDtypeStruct((NT, D_B), F32),
                   jax.ShapeDtypeStruct((NT, D_C), F32), jax.ShapeDtypeStruct((NT, D_C), F32),
                   jax.ShapeDtypeStruct((NT, D_C), BF16), jax.ShapeDtypeStruct((NT, D_C), BF16),
                   jax.ShapeDtypeStruct((NT, D_C), BF16)],
        compiler_params=_params("parallel"),
        name="in_proj",
    )(h, norm_mix, w_in_b)


def _rg_gates(xc, wa, ba, wx, bx, lam):
    xcb = xc.astype(BF16)
    r = jax.nn.sigmoid(_dot(xcb, wa) + ba)
    ig = jax.nn.sigmoid(_dot(xcb, wx) + bx)
    log_a = RG_C * r * jax.nn.log_sigmoid(lam)
    a = jnp.exp(log_a)
    u = jnp.sqrt(1.0 - a * a) * (ig * xc)
    return a, u


def _pool_lane_select(lane, per_window):
    out = per_window[-1]
    for g in range(len(POOL_WINDOWS) - 2, -1, -1):
        out = jnp.where(lane < (g + 1) * POOL_GROUP, per_window[g], out)
    return out


def _mix_prompt_kernel(xag_ref, xb_ref, cw_ref, cb_ref, wa_ref, ba_ref, wx_ref, bx_ref, lam_ref,
                       na_ref, pw_ref, ps_ref,
                       ya_ref, yb_ref, convn_ref, rgn_ref, pooln_ref,
                       xbuf, pbuf, hcar):
    j = pl.program_id(1)
    hist_a = 8
    hist_b = 16

    @pl.when(j == 0)
    def _():
        xbuf[0:hist_a, :] = jnp.zeros((hist_a, D_A), F32)
        pbuf[0:hist_b, :] = jnp.zeros((hist_b, D_B), F32)
        hcar[...] = jnp.zeros_like(hcar)

    xa = xag_ref[:, 0:D_A]
    ga = xag_ref[:, D_A:2 * D_A]
    xbuf[hist_a:hist_a + TT, :] = xa
    cw = cw_ref[...]
    xc = cb_ref[...]
    for jj in range(CONV_WIDTH):
        lag = CONV_WIDTH - 1 - jj
        xc = xc + cw[jj:jj + 1, :] * xbuf[hist_a - lag:hist_a - lag + TT, :]
    convn_ref[0] = xbuf[hist_a + TT - (CONV_WIDTH - 1):hist_a + TT, :]
    xbuf[0:hist_a, :] = xbuf[TT:TT + hist_a, :]

    a, u = _rg_gates(xc, wa_ref[...], ba_ref[...], wx_ref[...], bx_ref[...], lam_ref[...])
    row = lax.broadcasted_iota(jnp.int32, (TT, D_A), 0)
    d = 1
    while d < TT:
        m = row >= d
        a_s = jnp.where(m, pltpu.roll(a, d, 0), 1.0)
        u_s = jnp.where(m, pltpu.roll(u, d, 0), 0.0)
        u = a * u_s + u
        a = a * a_s
        d *= 2
    hs = u + a * hcar[0:1, :]
    hcar[0:1, :] = hs[TT - 1:TT, :]
    rgn_ref[0] = hs[TT - 1:TT, :]
    ya_ref[...] = _rms(jax.nn.gelu(ga) * hs, na_ref[...])

    xb = xb_ref[...]
    pbuf[hist_b:hist_b + TT, :] = xb
    lane = lax.broadcasted_iota(jnp.int32, (TT, D_B), 1)
    pos = j * TT + lax.broadcasted_iota(jnp.int32, (TT, D_B), 0)
    s = xb
    sums = []
    for lag in range(1, max(POOL_WINDOWS)):
        s = s + pbuf[hist_b - lag:hist_b - lag + TT, :]
        if lag + 1 in POOL_WINDOWS:
            sums.append(s)
    sums = [xb] * (len(POOL_WINDOWS) - len(sums)) + sums
    wsum = _pool_lane_select(lane, sums)
    win = _pool_lane_select(lane, [jnp.full((TT, D_B), w, jnp.int32) for w in POOL_WINDOWS])
    cnt = jnp.minimum(win, pos + 1).astype(F32)
    pooled = (wsum / cnt - xb).astype(BF16)
    yb_ref[...] = _dot(pooled, pw_ref[...]) * ps_ref[...]
    pooln_ref[0] = pbuf[hist_b + TT - POOL_BUF:hist_b + TT, :]
    pbuf[0:hist_b, :] = pbuf[TT:TT + hist_b, :]


def _mix_prompt(xag, xb, lw, layer):
    nj = SEQ // TT
    row = lambda w: pl.BlockSpec((TT, w), lambda b, j: (b * nj + j, 0))
    vec = lambda w: _layer_spec((1, w), layer)
    state = lambda r, w: pl.BlockSpec((1, r, w), lambda b, j: (b, 0, 0))
    return pl.pallas_call(
        _mix_prompt_kernel,
        grid=(BATCH, nj),
        in_specs=[row(2 * D_A), row(D_B), _layer_spec((CONV_WIDTH, D_A), layer), vec(D_A),
                  _layer_spec((D_A, D_A), layer), vec(D_A), _layer_spec((D_A, D_A), layer), vec(D_A),
                  vec(D_A), vec(D_A), _layer_spec((D_B, D_B), layer), vec(D_B)],
        out_specs=[row(D_A), row(D_B), state(CONV_WIDTH - 1, D_A), state(1, D_A), state(POOL_BUF, D_B)],
        out_shape=[jax.ShapeDtypeStruct((NP, D_A), F32), jax.ShapeDtypeStruct((NP, D_B), F32),
                   jax.ShapeDtypeStruct((BATCH, CONV_WIDTH - 1, D_A), F32),
                   jax.ShapeDtypeStruct((BATCH, 1, D_A), F32),
                   jax.ShapeDtypeStruct((BATCH, POOL_BUF, D_B), F32)],
        scratch_shapes=[pltpu.VMEM((TT + 8, D_A), F32), pltpu.VMEM((TT + 16, D_B), F32),
                        pltpu.VMEM((8, D_A), F32)],
        compiler_params=_params("parallel", "arbitrary"),
        name="mix_prompt",
    )(xag, xb, lw["conv_w"], lw["conv_b"], lw["rg_wa"], lw["rg_ba"], lw["rg_wx"], lw["rg_bx"],
      lw["rg_lambda"], lw["norm_a"], lw["pool_w"], lw["pool_scale"])


def _mix_sample_kernel(xag_ref, xb_ref, sconv_ref, srg_ref, spool_ref,
                       cw_ref, cb_ref, wa_ref, ba_ref, wx_ref, bx_ref, lam_ref, na_ref, pw_ref, ps_ref,
                       ya_ref, yb_ref, convn_ref, rgn_ref, pooln_ref):
    nb = DEC_BATCH
    rows = lambda t: slice(t * nb, (t + 1) * nb)
    xe = [sconv_ref[0, t] for t in range(CONV_WIDTH - 1)] + [xag_ref[rows(t), 0:D_A] for t in range(DEC_SEQ)]
    cw = cw_ref[...]
    xcs = []
    for t in range(DEC_SEQ):
        y = cb_ref[...]
        for jj in range(CONV_WIDTH):
            y = y + cw[jj:jj + 1, :] * xe[t + jj]
        xcs.append(y)
    for t in range(CONV_WIDTH - 1):
        convn_ref[t] = xe[len(xe) - (CONV_WIDTH - 1) + t]
    xc = jnp.concatenate(xcs, axis=0)
    a, u = _rg_gates(xc, wa_ref[...], ba_ref[...], wx_ref[...], bx_ref[...], lam_ref[...])
    h = srg_ref[0]
    hs = []
    for t in range(DEC_SEQ):
        h = a[rows(t)] * h + u[rows(t)]
        hs.append(h)
    rgn_ref[...] = h
    ga = xag_ref[:, D_A:2 * D_A]
    ya_ref[...] = _rms(jax.nn.gelu(ga) * jnp.concatenate(hs, axis=0), na_ref[...])

    pe = [spool_ref[0, t] for t in range(POOL_BUF)] + [xb_ref[rows(t), :] for t in range(DEC_SEQ)]
    for t in range(POOL_BUF):
        pooln_ref[t] = pe[len(pe) - POOL_BUF + t]
    lane = lax.broadcasted_iota(jnp.int32, (nb, D_B), 1)
    pooled = []
    for t in range(DEC_SEQ):
        s = pe[POOL_BUF + t]
        means = []
        for lag in range(1, max(POOL_WINDOWS)):
            s = s + pe[POOL_BUF + t - lag]
            if lag + 1 in POOL_WINDOWS:
                means.append(s / float(min(lag + 1, PAST_LEN + t + 1)))
        pooled.append(_pool_lane_select(lane, means) - pe[POOL_BUF + t])
    pooled = jnp.concatenate(pooled, axis=0).astype(BF16)
    yb_ref[...] = _dot(pooled, pw_ref[...]) * ps_ref[...]


def _mix_sample(xag, xb, sconv_t, srg, spool_t, lw, layer):
    vec = lambda w: _layer_spec((1, w), layer)
    lay = lambda *s: pl.BlockSpec((1,) + s, lambda i: (layer,) + (0,) * len(s))
    full = lambda *s: pl.BlockSpec(s, lambda i: (0,) * len(s))
    return pl.pallas_call(
        _mix_sample_kernel,
        grid=(1,),
        in_specs=[pl.BlockSpec((NS, 2 * D_A), lambda i: (NPT, 0)), pl.BlockSpec((NS, D_B), lambda i: (NPT, 0)),
                  lay(CONV_WIDTH - 1, DEC_BATCH, D_A), lay(DEC_BATCH, D_A), lay(POOL_BUF, DEC_BATCH, D_B),
                  _layer_spec((CONV_WIDTH, D_A), layer), vec(D_A),
                  _layer_spec((D_A, D_A), layer), vec(D_A), _layer_spec((D_A, D_A), layer), vec(D_A),
                  vec(D_A), vec(D_A), _layer_spec((D_B, D_B), layer), vec(D_B)],
        out_specs=[full(NS, D_A), full(NS, D_B), full(CONV_WIDTH - 1, DEC_BATCH, D_A), full(DEC_BATCH, D_A),
                   full(POOL_BUF, DEC_BATCH, D_B)],
        out_shape=[jax.ShapeDtypeStruct((NS, D_A), F32), jax.ShapeDtypeStruct((NS, D_B), F32),
                   jax.ShapeDtypeStruct((CONV_WIDTH - 1, DEC_BATCH, D_A), F32),
                   jax.ShapeDtypeStruct((DEC_BATCH, D_A), F32),
                   jax.ShapeDtypeStruct((POOL_BUF, DEC_BATCH, D_B), F32)],
        compiler_params=_params("arbitrary"),
        name="mix_sample",
    )(xag, xb, sconv_t, srg, spool_t, lw["conv_w"], lw["conv_b"], lw["rg_wa"], lw["rg_ba"], lw["rg_wx"],
      lw["rg_bx"], lw["rg_lambda"], lw["norm_a"], lw["pool_w"], lw["pool_scale"])


def _attn_prompt_kernel(bias_ref, q_ref, k_ref, v_ref, o_ref):
    p = pl.program_id(1)
    i = pl.program_id(2)
    q = q_ref[...]
    lane_q = lax.broadcasted_iota(jnp.int32, (TQ, HEAD_PAIR), 1)
    r = lax.broadcasted_iota(jnp.int32, (TQ, TK), 0)
    c = lax.broadcasted_iota(jnp.int32, (TQ, TK), 1)
    causal = c < r
    upper = _strict_lower_ones(TK)
    accs = []
    for hh in range(2):
        qm = jnp.where((lane_q >= hh * HEAD_DIM) & (lane_q < (hh + 1) * HEAD_DIM), q, jnp.zeros_like(q))
        bias = bias_ref[2 * p + hh]

        def tile(j, acc, run, diag, qm=qm, bias=bias):
            start = pl.multiple_of(j * TK, TK)
            k = k_ref[pl.ds(start, TK), :]
            v = v_ref[pl.ds(start, TK), :]
            z = _dot_nt(qm, k) * SCALE + bias
            l = _neg_softplus(z)
            if diag:
                l = jnp.where(causal, l, 0.0)
            cs = _suffix_sums(l, upper)
            a = jnp.exp(z + l + (cs + run))
            if diag:
                a = jnp.where(causal, a, 0.0)
            acc = acc + _dot(a.astype(BF16), v)
            run = run + (cs[:, 0:1] + l[:, 0:1])
            return acc, run

        acc, run = tile(i, jnp.zeros((TQ, HEAD_PAIR), F32), jnp.zeros((TQ, 1), F32), True)
        acc, run = lax.fori_loop(0, i, lambda jj, cr: tile(i - 1 - jj, cr[0], cr[1], False), (acc, run))
        accs.append(acc)
    o_ref[...] = jnp.where(lane_q < HEAD_DIM, accs[0], accs[1])


def _attn_prompt(bias, qb, kb, vb):
    nq = SEQ // TQ
    return pl.pallas_call(
        _attn_prompt_kernel,
        grid=(BATCH, N_PAIRS, nq),
        in_specs=[pl.BlockSpec(memory_space=pltpu.SMEM),
                  pl.BlockSpec((TQ, HEAD_PAIR), lambda b, p, i: (b * nq + i, p)),
                  pl.BlockSpec((SEQ, HEAD_PAIR), lambda b, p, i: (b, p)),
                  pl.BlockSpec((SEQ, HEAD_PAIR), lambda b, p, i: (b, p))],
        out_specs=pl.BlockSpec((TQ, HEAD_PAIR), lambda b, p, i: (b * nq + i, p)),
        out_shape=jax.ShapeDtypeStruct((NP, D_C), F32),
        compiler_params=_params("parallel", "parallel", "arbitrary"),
        name="attn_prompt",
    )(bias, qb, kb, vb)


QROWS = 32


def _attn_decode_kernel(pt_ref, bias_ref, q8_ref, kn_ref, vn_ref, *refs):
    kp = refs[0:N_PAGES]
    vp = refs[N_PAGES:2 * N_PAGES]
    o_ref = refs[2 * N_PAGES]
    kn_s, vn_s = refs[2 * N_PAGES + 1:]
    b = pl.program_id(0)

    @pl.when(b == 0)
    def _():
        kn_s[...] = jnp.zeros_like(kn_s)
        vn_s[...] = jnp.zeros_like(vn_s)

    kn_s[0:DEC_SEQ, :] = kn_ref[0]
    vn_s[0:DEC_SEQ, :] = vn_ref[0]

    rq = lax.broadcasted_iota(jnp.int32, (QROWS, D_C), 0)
    cq = lax.broadcasted_iota(jnp.int32, (QROWS, D_C), 1)
    head_lanes = (cq >> 6) == (rq >> 2)
    q8 = q8_ref[0]
    qbd = jnp.where(head_lanes, jnp.concatenate([q8] * (QROWS // 8), axis=0), 0.0).astype(BF16)

    rr = lax.broadcasted_iota(jnp.int32, (QROWS, PAGE_SIZE), 0)
    cc = lax.broadcasted_iota(jnp.int32, (QROWS, PAGE_SIZE), 1)
    bias = jnp.zeros((QROWS, PAGE_SIZE), F32)
    for h in range(N_C_HEADS):
        bias = jnp.where((rr >> 2) == h, bias_ref[h], bias)
    new_valid = cc < (rr & 3)
    upper = _strict_lower_ones(PAGE_SIZE)

    acc = jnp.zeros((QROWS, D_C), F32)
    run = jnp.zeros((QROWS, 1), F32)
    for blk in range(N_PAGES, -1, -1):
        is_new = blk == N_PAGES
        k = (kn_s[...] if is_new else kp[blk][...]).astype(BF16)
        v = (vn_s[...] if is_new else vp[blk][...]).astype(BF16)
        z = _dot_nt(qbd, k) * SCALE + bias
        l = _neg_softplus(z)
        if is_new:
            l = jnp.where(new_valid, l, 0.0)
        cs = _suffix_sums(l, upper)
        a = jnp.exp(z + l + (cs + run))
        if is_new:
            a = jnp.where(new_valid, a, 0.0)
        acc = acc + _dot(a.astype(BF16), v)
        run = run + (cs[:, 0:1] + l[:, 0:1])

    acc = jnp.where(head_lanes, acc, 0.0)
    s = acc[0:8] + acc[8:16] + acc[16:24]
    s = s + pltpu.roll(s, 4, 0)
    o_ref[0] = s[0:DEC_SEQ]


def _attn_decode(page_table_flat, bias, q8, kn, vn, cache_k, cache_v, layer):
    def page_spec(pg):
        return pl.BlockSpec((None, None, PAGE_SIZE, D_C),
                            lambda b, pt: (layer, pt[b * N_PAGES + pg], 0, 0))

    seq = lambda r: pl.BlockSpec((1, r, D_C), lambda b, pt: (b, 0, 0))
    grid_spec = pltpu.PrefetchScalarGridSpec(
        num_scalar_prefetch=1,
        grid=(DEC_BATCH,),
        in_specs=[pl.BlockSpec(memory_space=pltpu.SMEM), seq(8), seq(DEC_SEQ), seq(DEC_SEQ)]
        + [page_spec(pg) for pg in range(N_PAGES)] * 2,
        out_specs=seq(DEC_SEQ),
        scratch_shapes=[pltpu.VMEM((PAGE_SIZE, D_C), F32), pltpu.VMEM((PAGE_SIZE, D_C), F32)],
    )
    return pl.pallas_call(
        _attn_decode_kernel,
        grid_spec=grid_spec,
        out_shape=jax.ShapeDtypeStruct((DEC_BATCH, DEC_SEQ, D_C), F32),
        compiler_params=_params("arbitrary"),
        name="attn_decode",
    )(page_table_flat, bias, q8, kn, vn, *([cache_k] * N_PAGES), *([cache_v] * N_PAGES))


def _out_proj_kernel(yap_ref, ybp_ref, op_ref, yas_ref, ybs_ref, os_ref, h_ref, nc_ref, w_ref, o_ref):
    is_sample = pl.program_id(0) == NPT
    ya = jnp.where(is_sample, yas_ref[...], yap_ref[...])
    yb = jnp.where(is_sample, ybs_ref[...], ybp_ref[...])
    yc = _rms(jnp.where(is_sample, os_ref[...], op_ref[...]), nc_ref[...])
    y = jnp.concatenate([ya, yb, yc], axis=-1).astype(BF16)
    o_ref[...] = h_ref[...] + _dot(y, w_ref[...])


def _out_proj(ya_p, yb_p, o_p, ya_s, yb_s, o_s, h, norm_c, w_out_b, layer):
    prow = lambda w: pl.BlockSpec((TM, w), lambda i: (jnp.minimum(i, NPT - 1), 0))
    srow = lambda w: pl.BlockSpec((NS, w), lambda i: (0, 0))
    row = lambda w: pl.BlockSpec((TM, w), lambda i: (i, 0))
    return pl.pallas_call(
        _out_proj_kernel,
        grid=(NTT,),
        in_specs=[prow(D_A), prow(D_B), prow(D_C), srow(D_A), srow(D_B), srow(D_C), row(D_MODEL),
                  _layer_spec((1, D_C), layer), _layer_spec((D_MODEL, D_MODEL), layer)],
        out_specs=row(D_MODEL),
        out_shape=jax.ShapeDtypeStruct((NT, D_MODEL), F32),
        compiler_params=_params("parallel"),
        name="out_proj",
    )(ya_p, yb_p, o_p, ya_s, yb_s, o_s, h, norm_c, w_out_b)


def _ffn_kernel(h_ref, g_ref, wg_ref, wu_ref, wd_ref, o_ref):
    h = h_ref[...]
    hn = _rms(h, g_ref[...]).astype(BF16)
    act = (jax.nn.silu(_dot(hn, wg_ref[...])) * _dot(hn, wu_ref[...])).astype(BF16)
    o_ref[...] = h + _dot(act, wd_ref[...])


def _ffn(h, norm_ffn, wg_b, wu_b, wd_b, layer):
    row = pl.BlockSpec((TM, D_MODEL), lambda i: (i, 0))
    return pl.pallas_call(
        _ffn_kernel,
        grid=(NTT,),
        in_specs=[row, _layer_spec((1, D_MODEL), layer), _layer_spec((D_MODEL, D_FF), layer),
                  _layer_spec((D_MODEL, D_FF), layer), _layer_spec((D_FF, D_MODEL), layer)],
        out_specs=row,
        out_shape=jax.ShapeDtypeStruct((NT, D_MODEL), F32),
        compiler_params=_params("parallel"),
        name="ffn",
    )(h, norm_ffn, wg_b, wu_b, wd_b)


def _ple_kernel(h_ref, pe_ref, g_ref, wgate_ref, wproj_ref, o_ref):
    h = h_ref[...]
    hn = _rms(h, g_ref[...]).astype(BF16)
    gate = jax.nn.sigmoid(_dot(hn, wgate_ref[...]))
    o_ref[...] = h + gate * _dot(pe_ref[...].astype(BF16), wproj_ref[...])


def _ple(h, pe, norm_ple, wgate_b, wproj_b, layer):
    row = pl.BlockSpec((TM, D_MODEL), lambda i: (i, 0))
    return pl.pallas_call(
        _ple_kernel,
        grid=(NTT,),
        in_specs=[row, pl.BlockSpec((None, TM, D_PLE), lambda i: (layer, i, 0)),
                  _layer_spec((1, D_MODEL), layer), _layer_spec((D_MODEL, D_MODEL), layer),
                  _layer_spec((D_PLE, D_MODEL), layer)],
        out_specs=row,
        out_shape=jax.ShapeDtypeStruct((NT, D_MODEL), F32),
        compiler_params=_params("parallel"),
        name="ple",
    )(h, pe, norm_ple, wgate_b, wproj_b)


def _final_norm_kernel(h_ref, g_ref, o_ref):
    o_ref[...] = _rms(h_ref[...], g_ref[...])


def _final_norm(h, g):
    row = pl.BlockSpec((TM, D_MODEL), lambda i: (i, 0))
    return pl.pallas_call(
        _final_norm_kernel,
        grid=(NTT,),
        in_specs=[row, pl.BlockSpec((1, D_MODEL), lambda i: (0, 0))],
        out_specs=row,
        out_shape=jax.ShapeDtypeStruct((NT, D_MODEL), F32),
        compiler_params=_params("parallel"),
        name="final_norm",
    )(h, g)


def _block_diag(w):
    depth, g, n, _ = w.shape
    eye = jnp.eye(g, dtype=w.dtype)
    return jnp.einsum("dgij,gh->dgihj", w, eye).reshape(depth, g * n, g * n)


def kernel(x_prompt, x_sample, cache_k, cache_v, state_conv, state_rglru, state_pool, page_table,
           p_prompt, p_sample, norm_mix, w_in, conv_w, conv_b, rg_wa, rg_ba, rg_wx, rg_bx, rg_lambda,
           norm_a, pool_w, pool_scale, norm_c, sb_bias, w_out, norm_ffn, w_ff_gate, w_ff_up, w_ff_down,
           norm_ple, w_ple_gate, w_ple_proj, norm_final):
    n_pool = cache_k.shape[1]
    tmajor = lambda x: jnp.swapaxes(x, -3, -2)

    h = jnp.concatenate([x_prompt.reshape(NP, D_MODEL), tmajor(x_sample).reshape(NS, D_MODEL)], axis=0)
    pe = jnp.concatenate([p_prompt.reshape(DEPTH, NP, D_PLE), tmajor(p_sample).reshape(DEPTH, NS, D_PLE)], axis=1)
    ck = cache_k.reshape(DEPTH, n_pool, PAGE_SIZE, D_C)
    cv = cache_v.reshape(DEPTH, n_pool, PAGE_SIZE, D_C)
    pt = page_table.reshape(-1)
    sconv_t = tmajor(state_conv)
    spool_t = tmajor(state_pool)

    vec = lambda x: x.reshape(DEPTH, 1, -1)
    lw = {
        "conv_w": conv_w, "conv_b": vec(conv_b),
        "rg_wa": _block_diag(rg_wa).astype(BF16), "rg_ba": vec(rg_ba),
        "rg_wx": _block_diag(rg_wx).astype(BF16), "rg_bx": vec(rg_bx),
        "rg_lambda": vec(rg_lambda), "norm_a": vec(norm_a),
        "pool_w": _block_diag(pool_w).astype(BF16), "pool_scale": vec(pool_scale),
    }
    w_in_b = w_in.astype(BF16)
    w_out_b = w_out.astype(BF16)
    wg_b = w_ff_gate.astype(BF16)
    wu_b = w_ff_up.astype(BF16)
    wd_b = w_ff_down.astype(BF16)
    wpg_b = w_ple_gate.astype(BF16)
    wpp_b = w_ple_proj.astype(BF16)
    norm_mix3, norm_c3, norm_ffn3, norm_ple3 = vec(norm_mix), vec(norm_c), vec(norm_ffn), vec(norm_ple)

    outs = {n: [] for n in ("kp", "vp", "ks", "vs", "cp", "cs", "rp", "rs", "pp", "ps")}
    for l in range(DEPTH):
        xag, xb, k, v, qb, kb, vb = _in_proj(h, norm_mix3, w_in_b, l)
        ya_p, yb_p, conv_p, rg_p, pool_p = _mix_prompt(xag, xb, lw, l)
        ya_s, yb_s, conv_s, rg_s, pool_s = _mix_sample(xag, xb, sconv_t, state_rglru, spool_t, lw, l)
        o_p = _attn_prompt(sb_bias[l], qb, kb, vb)

        bmajor = lambda x: tmajor(x[NP:].reshape(DEC_SEQ, DEC_BATCH, D_C))
        q_s = bmajor(qb).astype(F32)
        k_s = bmajor(k)
        v_s = bmajor(v)
        o_s = _attn_decode(pt, sb_bias[l], jnp.concatenate([q_s, q_s], axis=1), k_s, v_s, ck, cv, l)
        o_s = tmajor(o_s).reshape(NS, D_C)

        h = _out_proj(ya_p, yb_p, o_p, ya_s, yb_s, o_s, h, norm_c3, w_out_b, l)
        h = _ffn(h, norm_ffn3, wg_b, wu_b, wd_b, l)
        h = _ple(h, pe, norm_ple3, wpg_b, wpp_b, l)

        outs["kp"].append(k[:NP].reshape(BATCH, SEQ, N_C_HEADS, HEAD_DIM))
        outs["vp"].append(v[:NP].reshape(BATCH, SEQ, N_C_HEADS, HEAD_DIM))
        outs["ks"].append(k_s.reshape(DEC_BATCH, DEC_SEQ, N_C_HEADS, HEAD_DIM))
        outs["vs"].append(v_s.reshape(DEC_BATCH, DEC_SEQ, N_C_HEADS, HEAD_DIM))
        outs["cp"].append(conv_p)
        outs["cs"].append(tmajor(conv_s))
        outs["rp"].append(rg_p.reshape(BATCH, D_A))
        outs["rs"].append(rg_s)
        outs["pp"].append(pool_p)
        outs["ps"].append(tmajor(pool_s))

    y = _final_norm(h, norm_final.reshape(1, D_MODEL))
    y_prompt = y[:NP].reshape(BATCH, SEQ, D_MODEL)
    y_sample = tmajor(y[NP:].reshape(DEC_SEQ, DEC_BATCH, D_MODEL))
    st = lambda n: jnp.stack(outs[n])
    return (y_prompt, y_sample, st("kp"), st("vp"), st("ks"), st("vs"),
            st("cp"), st("cs"), st("rp"), st("rs"), st("pp"), st("ps"))
```

```python
import jax
import jax.numpy as jnp
from jax import lax
from jax.experimental import pallas as pl
from jax.experimental.pallas import tpu as pltpu

F32 = jnp.float32
BF16 = jnp.bfloat16

D_MODEL = 1024
BATCH = 4
SEQ = 4096
DEPTH = 4
DEC_BATCH = 128
DEC_SEQ = 4
PAST_LEN = 2048
PAGE_SIZE = 128
N_PAGES = PAST_LEN // PAGE_SIZE
HEAD_DIM = 64
D_A = 384
D_B = 256
D_C = 384
N_C_HEADS = 6
POOL_WINDOWS = (2, 4, 8, 16)
POOL_GROUP = 64
POOL_BUF = 15
CONV_WIDTH = 4
RG_C = 8.0
D_IN = 2 * D_A + D_B + 3 * D_C
D_FF = 2816
D_PLE = 256
EPS = 1e-6
SCALE = HEAD_DIM ** -0.5

NP = BATCH * SEQ
NS = DEC_BATCH * DEC_SEQ
NT = NP + NS
TM = 512
NPT = NP // TM
NTT = NT // TM
TT = 512
TQ = 512
TK = 256
LANE = 128
HEAD_PAIR = 2 * HEAD_DIM
N_PAIRS = D_C // HEAD_PAIR
VMEM_LIMIT = 56 * 1024 * 1024


def _rms(x, g):
    ms = jnp.mean(x * x, axis=-1, keepdims=True)
    return x * lax.rsqrt(ms + EPS) * g


def _dot(a, b):
    return lax.dot_general(a, b, (((1,), (0,)), ((), ())), preferred_element_type=F32)


def _dot_nt(a, b):
    return lax.dot_general(a, b, (((1,), (1,)), ((), ())), preferred_element_type=F32)


def _softplus(z):
    neg_abs = pltpu.bitcast(pltpu.bitcast(z, jnp.uint32) | jnp.uint32(0x80000000), F32)
    return jnp.maximum(z, 0.0) + jnp.log(1.0 + jnp.exp(neg_abs))


def _hi_lo(x):
    hi = pltpu.bitcast(pltpu.bitcast(x, jnp.uint32) & jnp.uint32(0xFFFF0000), F32)
    return jnp.concatenate([hi, x - hi], axis=1)


def _suffix_ones(n):
    r = lax.broadcasted_iota(jnp.int32, (2 * n, n), 0) & (n - 1)
    c = lax.broadcasted_iota(jnp.int32, (2 * n, n), 1)
    return jnp.where(r >= c, 1.0, 0.0).astype(BF16)


def _layer_spec(shape, layer):
    nd = len(shape)
    return pl.BlockSpec((None,) + tuple(shape), lambda *_: (layer,) + (0,) * nd,
                        pipeline_mode=pl.Buffered(1))


def _params(*sem):
    return pltpu.CompilerParams(dimension_semantics=sem, vmem_limit_bytes=VMEM_LIMIT)


S_XB = 2 * D_A
S_Q = S_XB + D_B
S_K = S_Q + D_C
S_V = S_K + D_C


def _in_proj_kernel(h_ref, g_ref, w_ref, wkvt_ref, kt_prev, vt_prev,
                    xag_ref, xb_ref, qb_ref, vb_ref, ktb_ref, kt_ref, vt_ref, qkvs_ref, kts_ref, vts_ref):
    del kt_prev, vt_prev
    i = pl.program_id(0)
    hn = _rms(h_ref[...], g_ref[...]).astype(BF16)
    u = _dot(hn, w_ref[...])
    kvt = _dot_nt(wkvt_ref[...], hn)
    xag_ref[...] = u[:, :S_XB]
    xb_ref[...] = u[:, S_XB:S_Q]

    @pl.when(i < NPT)
    def _():
        qb_ref[...] = u[:, S_Q:S_K].astype(BF16)
        vb_ref[...] = u[:, S_V:].astype(BF16)
        kt = kvt[:D_C]
        kt_ref[...] = kt
        ktb_ref[...] = kt.astype(BF16)
        vt_ref[...] = kvt[D_C:]

    @pl.when(i == NPT)
    def _():
        qkvs_ref[...] = u[:, S_Q:]
        for t in range(DEC_SEQ):
            kts_ref[t] = kvt[:D_C, t * DEC_BATCH:(t + 1) * DEC_BATCH]
            vts_ref[t] = kvt[D_C:, t * DEC_BATCH:(t + 1) * DEC_BATCH]


def _in_proj(h, norm_mix, w_in_b, w_kvt_b, kt_all, vt_all, layer):
    tiles_per_seq = SEQ // TM
    pi = lambda i: jnp.minimum(i, NPT - 1)
    row = lambda w: pl.BlockSpec((TM, w), lambda i: (i, 0))
    prow = lambda w: pl.BlockSpec((TM, w), lambda i: (pi(i), 0))
    full = lambda *s: pl.BlockSpec(s, lambda i: (0,) * len(s))
    kv_t = pl.BlockSpec((None, None, D_C, TM), lambda i: (layer, pi(i) // tiles_per_seq, 0, pi(i) % tiles_per_seq))
    ktb = pl.BlockSpec((None, D_C, TM), lambda i: (pi(i) // tiles_per_seq, 0, pi(i) % tiles_per_seq))
    any_spec = pl.BlockSpec(memory_space=pl.ANY)
    kv_shape = jax.ShapeDtypeStruct((DEPTH, BATCH, D_C, SEQ), F32)
    return pl.pallas_call(
        _in_proj_kernel,
        grid=(NTT,),
        in_specs=[row(D_MODEL), _layer_spec((1, D_MODEL), layer), _layer_spec((D_MODEL, D_IN), layer),
                  _layer_spec((2 * D_C, D_MODEL), layer), any_spec, any_spec],
        out_specs=[row(2 * D_A), row(D_B), prow(D_C), prow(D_C), ktb, kv_t, kv_t,
                   full(NS, 3 * D_C), full(DEC_SEQ, D_C, DEC_BATCH), full(DEC_SEQ, D_C, DEC_BATCH)],
        out_shape=[jax.ShapeDtypeStruct((NT, 2 * D_A), F32), jax.ShapeDtypeStruct((NT, D_B), F32),
                   jax.ShapeDtypeStruct((NP, D_C), BF16), jax.ShapeDtypeStruct((NP, D_C), BF16),
                   jax.ShapeDtypeStruct((BATCH, D_C, SEQ), BF16), kv_shape, kv_shape,
                   jax.ShapeDtypeStruct((NS, 3 * D_C), F32),
                   jax.ShapeDtypeStruct((DEC_SEQ, D_C, DEC_BATCH), F32),
                   jax.ShapeDtypeStruct((DEC_SEQ, D_C, DEC_BATCH), F32)],
        input_output_aliases={4: 5, 5: 6},
        compiler_params=_params("arbitrary"),
        name="in_proj",
    )(h, norm_mix, w_in_b, w_kvt_b, kt_all, vt_all)


def _rg_gates(xc, wa, ba, wx, bx, lam):
    xcb = xc.astype(BF16)
    r = jax.nn.sigmoid(_dot(xcb, wa) + ba)
    ig = jax.nn.sigmoid(_dot(xcb, wx) + bx)
    log_a = RG_C * r * jax.nn.log_sigmoid(lam)
    a = jnp.exp(log_a)
    om = 1.0 - a * a
    u = jnp.where(om > 0.0, om * lax.rsqrt(om), 0.0) * (ig * xc)
    return a, u


def _pool_lane_select(lane, per_window):
    out = per_window[-1]
    for g in range(len(POOL_WINDOWS) - 2, -1, -1):
        out = jnp.where(lane < (g + 1) * POOL_GROUP, per_window[g], out)
    return out


def _mix_prompt_kernel(xag_ref, xb_ref, cw_ref, cb_ref, wa_ref, ba_ref, wx_ref, bx_ref, lam_ref,
                       na_ref, pw_ref, ps_ref,
                       ya_ref, yb_ref, convn_ref, rgn_ref, pooln_ref,
                       xbuf, pbuf, hcar, hs_s):
    j = pl.program_id(1)
    hist_a = 8
    hist_b = 16

    @pl.when(j == 0)
    def _():
        xbuf[0:hist_a, :] = jnp.zeros((hist_a, D_A), F32)
        pbuf[0:hist_b, :] = jnp.zeros((hist_b, D_B), F32)
        hcar[...] = jnp.zeros_like(hcar)

    xa = xag_ref[:, 0:D_A]
    ga = xag_ref[:, D_A:2 * D_A]
    xbuf[hist_a:hist_a + TT, :] = xa
    cw = cw_ref[...]
    assert CONV_WIDTH == 4
    e = xbuf[...]
    e1 = pltpu.roll(e, 1, 0)
    p2 = pltpu.roll(cw[1:2, :] * e + cw[0:1, :] * e1, 2, 0)
    xc = (cb_ref[...] + cw[3:4, :] * e + cw[2:3, :] * e1 + p2)[hist_a:hist_a + TT]
    convn_ref[0] = xbuf[hist_a + TT - (CONV_WIDTH - 1):hist_a + TT, :]
    xbuf[0:hist_a, :] = xbuf[TT:TT + hist_a, :]

    a, u = _rg_gates(xc, wa_ref[...], ba_ref[...], wx_ref[...], bx_ref[...], lam_ref[...])
    sub_rows = 8
    a3 = a.reshape(TT // sub_rows, sub_rows, D_A)
    u3 = u.reshape(TT // sub_rows, sub_rows, D_A)
    sub = lax.broadcasted_iota(jnp.int32, a3.shape, 1)
    d = 1
    while d < sub_rows:
        m = sub >= d
        a_s = jnp.where(m, pltpu.roll(a3, d, 1), 1.0)
        u_s = jnp.where(m, pltpu.roll(u3, d, 1), 0.0)
        u3 = a3 * u_s + u3
        a3 = a3 * a_s
        d *= 2
    h = hcar[0:1, :]
    for g in range(TT // sub_rows):
        hs_g = u3[g] + a3[g] * h
        hs_s[g * sub_rows:(g + 1) * sub_rows, :] = hs_g
        h = hs_g[sub_rows - 1:sub_rows, :]
    hcar[0:1, :] = h
    rgn_ref[0] = h
    hs = hs_s[...]
    ya_ref[...] = _rms(jax.nn.gelu(ga) * hs, na_ref[...])

    xb = xb_ref[...]
    pbuf[hist_b:hist_b + TT, :] = xb
    lane = lax.broadcasted_iota(jnp.int32, (TT, D_B), 1)
    pos = j * TT + lax.broadcasted_iota(jnp.int32, (TT, D_B), 0)
    assert POOL_WINDOWS == (2, 4, 8, 16) and D_B == 2 * LANE and 2 * POOL_GROUP == LANE
    e = pbuf[...]
    s2 = e + pltpu.roll(e, 1, 0)
    s4 = s2 + pltpu.roll(s2, 2, 0)
    s4_hi = s4[:, LANE:]
    s8_hi = s4_hi + pltpu.roll(s4_hi, 4, 0)
    s16_hi = s8_hi + pltpu.roll(s8_hi, 8, 0)
    tile = lambda x: x[hist_b:hist_b + TT]
    lane_h = lax.broadcasted_iota(jnp.int32, (TT, LANE), 1)
    wsum = jnp.concatenate(
        [jnp.where(lane_h < POOL_GROUP, tile(s2[:, :LANE]), tile(s4[:, :LANE])),
         jnp.where(lane_h < POOL_GROUP, tile(s8_hi), tile(s16_hi))], axis=1)
    win = _pool_lane_select(lane, [jnp.full((TT, D_B), w, jnp.int32) for w in POOL_WINDOWS])
    cnt = jnp.minimum(win, pos + 1).astype(F32)
    pooled = (wsum / cnt - xb).astype(BF16)
    yb_ref[...] = _dot(pooled, pw_ref[...]) * ps_ref[...]
    pooln_ref[0] = pbuf[hist_b + TT - POOL_BUF:hist_b + TT, :]
    pbuf[0:hist_b, :] = pbuf[TT:TT + hist_b, :]


def _mix_prompt(xag, xb, lw, layer):
    nj = SEQ // TT
    row = lambda w: pl.BlockSpec((TT, w), lambda b, j: (b * nj + j, 0))
    vec = lambda w: _layer_spec((1, w), layer)
    state = lambda r, w: pl.BlockSpec((1, r, w), lambda b, j: (b, 0, 0))
    return pl.pallas_call(
        _mix_prompt_kernel,
        grid=(BATCH, nj),
        in_specs=[row(2 * D_A), row(D_B), _layer_spec((CONV_WIDTH, D_A), layer), vec(D_A),
                  _layer_spec((D_A, D_A), layer), vec(D_A), _layer_spec((D_A, D_A), layer), vec(D_A),
                  vec(D_A), vec(D_A), _layer_spec((D_B, D_B), layer), vec(D_B)],
        out_specs=[row(D_A), row(D_B), state(CONV_WIDTH - 1, D_A), state(1, D_A), state(POOL_BUF, D_B)],
        out_shape=[jax.ShapeDtypeStruct((NP, D_A), F32), jax.ShapeDtypeStruct((NP, D_B), F32),
                   jax.ShapeDtypeStruct((BATCH, CONV_WIDTH - 1, D_A), F32),
                   jax.ShapeDtypeStruct((BATCH, 1, D_A), F32),
                   jax.ShapeDtypeStruct((BATCH, POOL_BUF, D_B), F32)],
        scratch_shapes=[pltpu.VMEM((TT + 8, D_A), F32), pltpu.VMEM((TT + 16, D_B), F32),
                        pltpu.VMEM((8, D_A), F32), pltpu.VMEM((TT, D_A), F32)],
        compiler_params=_params("parallel", "arbitrary"),
        name="mix_prompt",
    )(xag, xb, lw["conv_w"], lw["conv_b"], lw["rg_wa"], lw["rg_ba"], lw["rg_wx"], lw["rg_bx"],
      lw["rg_lambda"], lw["norm_a"], lw["pool_w"], lw["pool_scale"])


def _mix_sample_kernel(xag_ref, xb_ref, sconv_ref, srg_ref, spool_ref,
                       cw_ref, cb_ref, wa_ref, ba_ref, wx_ref, bx_ref, lam_ref, na_ref, pw_ref, ps_ref,
                       ya_ref, yb_ref, convn_ref, rgn_ref, pooln_ref):
    nb = DEC_BATCH
    rows = lambda t: slice(t * nb, (t + 1) * nb)
    xe = [sconv_ref[0, t] for t in range(CONV_WIDTH - 1)] + [xag_ref[rows(t), 0:D_A] for t in range(DEC_SEQ)]
    cw = cw_ref[...]
    xcs = []
    for t in range(DEC_SEQ):
        y = cb_ref[...]
        for jj in range(CONV_WIDTH):
            y = y + cw[jj:jj + 1, :] * xe[t + jj]
        xcs.append(y)
    for t in range(CONV_WIDTH - 1):
        convn_ref[t] = xe[len(xe) - (CONV_WIDTH - 1) + t]
    xc = jnp.concatenate(xcs, axis=0)
    a, u = _rg_gates(xc, wa_ref[...], ba_ref[...], wx_ref[...], bx_ref[...], lam_ref[...])
    h = srg_ref[0]
    hs = []
    for t in range(DEC_SEQ):
        h = a[rows(t)] * h + u[rows(t)]
        hs.append(h)
    rgn_ref[...] = h
    ga = xag_ref[:, D_A:2 * D_A]
    ya_ref[...] = _rms(jax.nn.gelu(ga) * jnp.concatenate(hs, axis=0), na_ref[...])

    pe = [spool_ref[0, t] for t in range(POOL_BUF)] + [xb_ref[rows(t), :] for t in range(DEC_SEQ)]
    for t in range(POOL_BUF):
        pooln_ref[t] = pe[len(pe) - POOL_BUF + t]
    lane = lax.broadcasted_iota(jnp.int32, (nb, D_B), 1)
    pooled = []
    for t in range(DEC_SEQ):
        s = pe[POOL_BUF + t]
        means = []
        for lag in range(1, max(POOL_WINDOWS)):
            s = s + pe[POOL_BUF + t - lag]
            if lag + 1 in POOL_WINDOWS:
                means.append(s / float(min(lag + 1, PAST_LEN + t + 1)))
        pooled.append(_pool_lane_select(lane, means) - pe[POOL_BUF + t])
    pooled = jnp.concatenate(pooled, axis=0).astype(BF16)
    yb_ref[...] = _dot(pooled, pw_ref[...]) * ps_ref[...]


def _mix_sample(xag, xb, sconv_t, srg, spool_t, lw, layer):
    vec = lambda w: _layer_spec((1, w), layer)
    lay = lambda *s: pl.BlockSpec((1,) + s, lambda i: (layer,) + (0,) * len(s))
    full = lambda *s: pl.BlockSpec(s, lambda i: (0,) * len(s))
    return pl.pallas_call(
        _mix_sample_kernel,
        grid=(1,),
        in_specs=[pl.BlockSpec((NS, 2 * D_A), lambda i: (NPT, 0)), pl.BlockSpec((NS, D_B), lambda i: (NPT, 0)),
                  lay(CONV_WIDTH - 1, DEC_BATCH, D_A), lay(DEC_BATCH, D_A), lay(POOL_BUF, DEC_BATCH, D_B),
                  _layer_spec((CONV_WIDTH, D_A), layer), vec(D_A),
                  _layer_spec((D_A, D_A), layer), vec(D_A), _layer_spec((D_A, D_A), layer), vec(D_A),
                  vec(D_A), vec(D_A), _layer_spec((D_B, D_B), layer), vec(D_B)],
        out_specs=[full(NS, D_A), full(NS, D_B), full(CONV_WIDTH - 1, DEC_BATCH, D_A), full(DEC_BATCH, D_A),
                   full(POOL_BUF, DEC_BATCH, D_B)],
        out_shape=[jax.ShapeDtypeStruct((NS, D_A), F32), jax.ShapeDtypeStruct((NS, D_B), F32),
                   jax.ShapeDtypeStruct((CONV_WIDTH - 1, DEC_BATCH, D_A), F32),
                   jax.ShapeDtypeStruct((DEC_BATCH, D_A), F32),
                   jax.ShapeDtypeStruct((POOL_BUF, DEC_BATCH, D_B), F32)],
        compiler_params=_params("arbitrary"),
        name="mix_sample",
    )(xag, xb, sconv_t, srg, spool_t, lw["conv_w"], lw["conv_b"], lw["rg_wa"], lw["rg_ba"], lw["rg_wx"],
      lw["rg_bx"], lw["rg_lambda"], lw["norm_a"], lw["pool_w"], lw["pool_scale"])


def _attn_prompt_kernel(bias_ref, q_ref, kt_ref, v_ref, o_ref, za_s, zb_s, acc_s, run_s):
    assert TQ == 2 * TK
    p = pl.program_id(1)
    i = pl.program_id(2)
    q = q_ref[...]
    lane_q = lax.broadcasted_iota(jnp.int32, (TQ, HEAD_PAIR), 1)
    qs = jnp.concatenate([jnp.where(lane_q < HEAD_DIM, q, jnp.zeros_like(q)),
                          jnp.where(lane_q >= HEAD_DIM, q, jnp.zeros_like(q))], axis=0) * jnp.asarray(SCALE, BF16)
    r = lax.broadcasted_iota(jnp.int32, (2 * TQ, TK), 0) & (TQ - 1)
    c = lax.broadcasted_iota(jnp.int32, (2 * TQ, TK), 1)
    ones = _suffix_ones(TK)
    bias0 = bias_ref[2 * p]
    bias1 = bias_ref[2 * p + 1]

    def scores(j):
        start = pl.multiple_of(j * TK, TK)
        z = _dot(qs, kt_ref[:, pl.ds(start, TK)])
        return jnp.concatenate([z[:TQ] + bias0, z[TQ:] + bias1], axis=0)

    def process(z_ref, j, diag_off):
        z = z_ref[...]
        start = pl.multiple_of(j * TK, TK)
        sp = _softplus(z)
        if diag_off is not None:
            causal = c + diag_off * TK < r
            sp = jnp.where(causal, sp, 0.0)
        run = run_s[...]
        cs = _dot(_hi_lo(sp), ones)
        a = jnp.exp(z - (cs + jnp.concatenate([run] * (TK // LANE), axis=1)))
        if diag_off is not None:
            a = jnp.where(causal, a, 0.0)
        acc_s[...] += _dot(a, v_ref[pl.ds(start, TK), :])
        run_s[...] = run + jnp.broadcast_to(cs[:, 0:1], run.shape)

    acc_s[...] = jnp.zeros_like(acc_s)
    run_s[...] = jnp.zeros_like(run_s)
    t0 = 2 * i + 1
    za_s[...] = scores(t0)
    zb_s[...] = scores(t0 - 1)
    process(za_s, t0, 1)
    za_s[...] = scores(jnp.maximum(t0 - 2, 0))
    process(zb_s, t0 - 1, 0)

    def two_tiles(jj, carry):
        t = t0 - 2 - 2 * jj
        zb_s[...] = scores(t - 1)
        process(za_s, t, None)
        za_s[...] = scores(jnp.maximum(t - 2, 0))
        process(zb_s, t - 1, None)
        return carry

    lax.fori_loop(0, i, two_tiles, 0)
    acc = acc_s[...]
    o_ref[...] = jnp.where(lane_q < HEAD_DIM, acc[:TQ], acc[TQ:])


def _attn_prompt(bias, qb, ktb, vb):
    nq = SEQ // TQ
    return pl.pallas_call(
        _attn_prompt_kernel,
        grid=(BATCH, N_PAIRS, nq),
        in_specs=[pl.BlockSpec(memory_space=pltpu.SMEM),
                  pl.BlockSpec((TQ, HEAD_PAIR), lambda b, p, i: (b * nq + i, p)),
                  pl.BlockSpec((None, HEAD_PAIR, SEQ), lambda b, p, i: (b, p, 0)),
                  pl.BlockSpec((SEQ, HEAD_PAIR), lambda b, p, i: (b, p))],
        out_specs=pl.BlockSpec((TQ, HEAD_PAIR), lambda b, p, i: (b * nq + i, p)),
        out_shape=jax.ShapeDtypeStruct((NP, D_C), F32),
        scratch_shapes=[pltpu.VMEM((2 * TQ, TK), F32), pltpu.VMEM((2 * TQ, TK), F32),
                        pltpu.VMEM((2 * TQ, HEAD_PAIR), F32), pltpu.VMEM((2 * TQ, HEAD_PAIR), F32)],
        compiler_params=_params("parallel", "parallel", "arbitrary"),
        name="attn_prompt",
    )(bias, qb, ktb, vb)


QROWS = 32
N_BLOCKS = N_PAGES + 1


def _attn_decode_kernel(pt_ref, bias_ref, q8_ref, kn_ref, vn_ref, *refs):
    del pt_ref
    ktp = refs[0:N_PAGES]
    vtp = refs[N_PAGES:2 * N_PAGES]
    o_ref = refs[2 * N_PAGES]
    kn_s, vn_s = refs[2 * N_PAGES + 1:]
    b = pl.program_id(0)

    @pl.when(b == 0)
    def _():
        kn_s[...] = jnp.zeros_like(kn_s)
        vn_s[...] = jnp.zeros_like(vn_s)

    kn_s[0:DEC_SEQ, :] = kn_ref[0]
    vn_s[0:DEC_SEQ, :] = vn_ref[0]

    rq = lax.broadcasted_iota(jnp.int32, (QROWS, D_C), 0)
    cq = lax.broadcasted_iota(jnp.int32, (QROWS, D_C), 1)
    head_lanes = (cq >> 6) == (rq >> 2)
    q8 = q8_ref[0] * SCALE
    qbd = jnp.where(head_lanes, jnp.concatenate([q8] * (QROWS // 8), axis=0), 0.0)

    rr = lax.broadcasted_iota(jnp.int32, (QROWS, PAGE_SIZE), 0)
    cc = lax.broadcasted_iota(jnp.int32, (QROWS, PAGE_SIZE), 1)
    bias = jnp.zeros((QROWS, PAGE_SIZE), F32)
    for h in range(N_C_HEADS):
        bias = jnp.where((rr >> 2) == h, bias_ref[h], bias)
    new_valid = cc < (rr & 3)

    zs = []
    for pp in range(N_PAGES // 2):
        kt2 = jnp.concatenate([ktp[2 * pp][...], ktp[2 * pp + 1][...]], axis=1)
        z2 = _dot(qbd, kt2)
        zs += [z2[:, :PAGE_SIZE] + bias, z2[:, PAGE_SIZE:] + bias]
    z_old = jnp.concatenate(zs, axis=0)
    z_new = _dot_nt(qbd, kn_s[...]) + bias
    sp = jnp.concatenate([_softplus(z_old), jnp.where(new_valid, _softplus(z_new), 0.0)], axis=0)
    z = jnp.concatenate([z_old, z_new], axis=0)
    cs = _dot(_hi_lo(sp), _suffix_ones(PAGE_SIZE))

    run = jnp.zeros((QROWS, 1), F32)
    runs = [None] * N_BLOCKS
    for n in range(N_BLOCKS - 1, -1, -1):
        runs[n] = jnp.broadcast_to(run, (QROWS, PAGE_SIZE))
        run = run + cs[n * QROWS:(n + 1) * QROWS, 0:1]
    a = jnp.exp(z - (cs + jnp.concatenate(runs, axis=0)))
    blk = lambda n: a[n * QROWS:(n + 1) * QROWS]

    acc = _dot(jnp.where(new_valid, blk(N_PAGES), 0.0), vn_s[...])
    for pp in range(N_PAGES // 2):
        a2 = jnp.concatenate([blk(2 * pp), blk(2 * pp + 1)], axis=1)
        vt2 = jnp.concatenate([vtp[2 * pp][...], vtp[2 * pp + 1][...]], axis=1)
        acc = acc + _dot_nt(a2, vt2)

    acc = jnp.where(head_lanes, acc, 0.0)
    s = acc[0:8] + acc[8:16] + acc[16:24]
    s = s + pltpu.roll(s, 4, 0)
    o_ref[0] = s[0:DEC_SEQ]


def _attn_decode(page_table_flat, bias, q8, kn, vn, cache_kt, cache_vt, layer):
    def page_spec(pg):
        return pl.BlockSpec((None, None, D_C, PAGE_SIZE),
                            lambda b, pt: (layer, pt[b * N_PAGES + pg], 0, 0))

    seq = lambda r: pl.BlockSpec((1, r, D_C), lambda b, pt: (b, 0, 0))
    grid_spec = pltpu.PrefetchScalarGridSpec(
        num_scalar_prefetch=1,
        grid=(DEC_BATCH,),
        in_specs=[pl.BlockSpec(memory_space=pltpu.SMEM), seq(8), seq(DEC_SEQ), seq(DEC_SEQ)]
        + [page_spec(pg) for pg in range(N_PAGES)] * 2,
        out_specs=seq(DEC_SEQ),
        scratch_shapes=[pltpu.VMEM((PAGE_SIZE, D_C), F32), pltpu.VMEM((PAGE_SIZE, D_C), F32)],
    )
    return pl.pallas_call(
        _attn_decode_kernel,
        grid_spec=grid_spec,
        out_shape=jax.ShapeDtypeStruct((DEC_BATCH, DEC_SEQ, D_C), F32),
        compiler_params=_params("arbitrary"),
        name="attn_decode",
    )(page_table_flat, bias, q8, kn, vn, *([cache_kt] * N_PAGES), *([cache_vt] * N_PAGES))


def _out_proj_kernel(yap_ref, ybp_ref, op_ref, yas_ref, ybs_ref, os_ref, h_ref, nc_ref, w_ref, o_ref):
    is_sample = pl.program_id(0) == NPT
    ya = jnp.where(is_sample, yas_ref[...], yap_ref[...])
    yb = jnp.where(is_sample, ybs_ref[...], ybp_ref[...])
    yc = _rms(jnp.where(is_sample, os_ref[...], op_ref[...]), nc_ref[...])
    y = jnp.concatenate([ya, yb, yc], axis=-1).astype(BF16)
    o_ref[...] = h_ref[...] + _dot(y, w_ref[...])


def _out_proj(ya_p, yb_p, o_p, ya_s, yb_s, o_s, h, norm_c, w_out_b, layer):
    prow = lambda w: pl.BlockSpec((TM, w), lambda i: (jnp.minimum(i, NPT - 1), 0))
    srow = lambda w: pl.BlockSpec((NS, w), lambda i: (0, 0))
    row = lambda w: pl.BlockSpec((TM, w), lambda i: (i, 0))
    return pl.pallas_call(
        _out_proj_kernel,
        grid=(NTT,),
        in_specs=[prow(D_A), prow(D_B), prow(D_C), srow(D_A), srow(D_B), srow(D_C), row(D_MODEL),
                  _layer_spec((1, D_C), layer), _layer_spec((D_MODEL, D_MODEL), layer)],
        out_specs=row(D_MODEL),
        out_shape=jax.ShapeDtypeStruct((NT, D_MODEL), F32),
        compiler_params=_params("parallel"),
        name="out_proj",
    )(ya_p, yb_p, o_p, ya_s, yb_s, o_s, h, norm_c, w_out_b)


def _ffn_kernel(h_ref, g_ref, wg_ref, wu_ref, wd_ref, o_ref):
    h = h_ref[...]
    hn = _rms(h, g_ref[...]).astype(BF16)
    act = (jax.nn.silu(_dot(hn, wg_ref[...])) * _dot(hn, wu_ref[...])).astype(BF16)
    o_ref[...] = h + _dot(act, wd_ref[...])


def _ffn(h, norm_ffn, wg_b, wu_b, wd_b, layer):
    row = pl.BlockSpec((TM, D_MODEL), lambda i: (i, 0))
    return pl.pallas_call(
        _ffn_kernel,
        grid=(NTT,),
        in_specs=[row, _layer_spec((1, D_MODEL), layer), _layer_spec((D_MODEL, D_FF), layer),
                  _layer_spec((D_MODEL, D_FF), layer), _layer_spec((D_FF, D_MODEL), layer)],
        out_specs=row,
        out_shape=jax.ShapeDtypeStruct((NT, D_MODEL), F32),
        compiler_params=_params("parallel"),
        name="ffn",
    )(h, norm_ffn, wg_b, wu_b, wd_b)


def _ple_kernel(h_ref, pe_ref, g_ref, wgate_ref, wproj_ref, o_ref):
    h = h_ref[...]
    hn = _rms(h, g_ref[...]).astype(BF16)
    gate = jax.nn.sigmoid(_dot(hn, wgate_ref[...]))
    o_ref[...] = h + gate * _dot(pe_ref[...].astype(BF16), wproj_ref[...])


def _ple(h, pe, norm_ple, wgate_b, wproj_b, layer):
    row = pl.BlockSpec((TM, D_MODEL), lambda i: (i, 0))
    return pl.pallas_call(
        _ple_kernel,
        grid=(NTT,),
        in_specs=[row, pl.BlockSpec((None, TM, D_PLE), lambda i: (layer, i, 0)),
                  _layer_spec((1, D_MODEL), layer), _layer_spec((D_MODEL, D_MODEL), layer),
                  _layer_spec((D_PLE, D_MODEL), layer)],
        out_specs=row,
        out_shape=jax.ShapeDtypeStruct((NT, D_MODEL), F32),
        compiler_params=_params("parallel"),
        name="ple",
    )(h, pe, norm_ple, wgate_b, wproj_b)


def _final_norm_kernel(h_ref, g_ref, o_ref):
    o_ref[...] = _rms(h_ref[...], g_ref[...])


def _final_norm(h, g):
    row = pl.BlockSpec((TM, D_MODEL), lambda i: (i, 0))
    return pl.pallas_call(
        _final_norm_kernel,
        grid=(NTT,),
        in_specs=[row, pl.BlockSpec((1, D_MODEL), lambda i: (0, 0))],
        out_specs=row,
        out_shape=jax.ShapeDtypeStruct((NT, D_MODEL), F32),
        compiler_params=_params("parallel"),
        name="final_norm",
    )(h, g)


def _block_diag(w):
    depth, g, n, _ = w.shape
    eye = jnp.eye(g, dtype=w.dtype)
    return jnp.einsum("dgij,gh->dgihj", w, eye).reshape(depth, g * n, g * n)


def kernel(x_prompt, x_sample, cache_k, cache_v, state_conv, state_rglru, state_pool, page_table,
           p_prompt, p_sample, norm_mix, w_in, conv_w, conv_b, rg_wa, rg_ba, rg_wx, rg_bx, rg_lambda,
           norm_a, pool_w, pool_scale, norm_c, sb_bias, w_out, norm_ffn, w_ff_gate, w_ff_up, w_ff_down,
           norm_ple, w_ple_gate, w_ple_proj, norm_final):
    n_pool = cache_k.shape[1]
    tmajor = lambda x: jnp.swapaxes(x, -3, -2)

    h = jnp.concatenate([x_prompt.reshape(NP, D_MODEL), tmajor(x_sample).reshape(NS, D_MODEL)], axis=0)
    pe = jnp.concatenate([p_prompt.reshape(DEPTH, NP, D_PLE), tmajor(p_sample).reshape(DEPTH, NS, D_PLE)], axis=1)
    ckt = jnp.transpose(cache_k, (0, 1, 3, 4, 2)).reshape(DEPTH, n_pool, D_C, PAGE_SIZE)
    cvt = jnp.transpose(cache_v, (0, 1, 3, 4, 2)).reshape(DEPTH, n_pool, D_C, PAGE_SIZE)
    pt = page_table.reshape(-1)
    sconv_t = tmajor(state_conv)
    spool_t = tmajor(state_pool)

    vec = lambda x: x.reshape(DEPTH, 1, -1)
    lw = {
        "conv_w": conv_w, "conv_b": vec(conv_b),
        "rg_wa": _block_diag(rg_wa).astype(BF16), "rg_ba": vec(rg_ba),
        "rg_wx": _block_diag(rg_wx).astype(BF16), "rg_bx": vec(rg_bx),
        "rg_lambda": vec(rg_lambda), "norm_a": vec(norm_a),
        "pool_w": _block_diag(pool_w).astype(BF16), "pool_scale": vec(pool_scale),
    }
    w_in_b = w_in.astype(BF16)
    w_kvt_b = jnp.swapaxes(w_in[:, :, S_K:], 1, 2).astype(BF16)
    w_out_b = w_out.astype(BF16)
    wg_b = w_ff_gate.astype(BF16)
    wu_b = w_ff_up.astype(BF16)
    wd_b = w_ff_down.astype(BF16)
    wpg_b = w_ple_gate.astype(BF16)
    wpp_b = w_ple_proj.astype(BF16)
    norm_mix3, norm_c3, norm_ffn3, norm_ple3 = vec(norm_mix), vec(norm_c), vec(norm_ffn), vec(norm_ple)

    kt_all = jnp.zeros((DEPTH, BATCH, D_C, SEQ), F32)
    vt_all = jnp.zeros((DEPTH, BATCH, D_C, SEQ), F32)
    outs = {n: [] for n in ("ks", "vs", "cp", "cs", "rp", "rs", "pp", "ps")}
    for l in range(DEPTH):
        xag, xb, qb, vb, ktb, kt_all, vt_all, qkv_s, kt_s, vt_s = _in_proj(
            h, norm_mix3, w_in_b, w_kvt_b, kt_all, vt_all, l)
        ya_p, yb_p, conv_p, rg_p, pool_p = _mix_prompt(xag, xb, lw, l)
        ya_s, yb_s, conv_s, rg_s, pool_s = _mix_sample(xag, xb, sconv_t, state_rglru, spool_t, lw, l)
        o_p = _attn_prompt(sb_bias[l], qb, ktb, vb)

        bmajor = lambda x: tmajor(x.reshape(DEC_SEQ, DEC_BATCH, D_C))
        q_s = bmajor(qkv_s[:, :D_C])
        o_s = _attn_decode(pt, sb_bias[l], jnp.concatenate([q_s, q_s], axis=1),
                           bmajor(qkv_s[:, D_C:2 * D_C]), bmajor(qkv_s[:, 2 * D_C:]), ckt, cvt, l)
        o_s = tmajor(o_s).reshape(NS, D_C)

        h = _out_proj(ya_p, yb_p, o_p, ya_s, yb_s, o_s, h, norm_c3, w_out_b, l)
        h = _ffn(h, norm_ffn3, wg_b, wu_b, wd_b, l)
        h = _ple(h, pe, norm_ple3, wpg_b, wpp_b, l)

        outs["ks"].append(kt_s)
        outs["vs"].append(vt_s)
        outs["cp"].append(conv_p)
        outs["cs"].append(conv_s)
        outs["rp"].append(rg_p.reshape(BATCH, D_A))
        outs["rs"].append(rg_s)
        outs["pp"].append(pool_p)
        outs["ps"].append(pool_s)

    y = _final_norm(h, norm_final.reshape(1, D_MODEL))
    y_prompt = y[:NP].reshape(BATCH, SEQ, D_MODEL)
    y_sample = tmajor(y[NP:].reshape(DEC_SEQ, DEC_BATCH, D_MODEL))
    st = lambda n: jnp.stack(outs[n])
    kv_prompt = lambda x: jnp.transpose(x.reshape(DEPTH, BATCH, N_C_HEADS, HEAD_DIM, SEQ), (0, 1, 4, 2, 3))
    kv_sample = lambda x: jnp.transpose(x.reshape(DEPTH, DEC_SEQ, N_C_HEADS, HEAD_DIM, DEC_BATCH), (0, 4, 1, 2, 3))
    return (y_prompt, y_sample, kv_prompt(kt_all), kv_prompt(vt_all), kv_sample(st("ks")), kv_sample(st("vs")),
            st("cp"), tmajor(st("cs")), st("rp"), st("rs"), st("pp"), tmajor(st("ps")))
```

```python
import functools

import jax
import jax.numpy as jnp
from jax import lax
from jax.experimental import pallas as pl
from jax.experimental.pallas import tpu as pltpu

F32 = jnp.float32
BF16 = jnp.bfloat16

D_MODEL = 1024
BATCH = 4
SEQ = 4096
DEPTH = 4
DEC_BATCH = 128
DEC_SEQ = 4
PAST_LEN = 2048
PAGE_SIZE = 128
N_PAGES = PAST_LEN // PAGE_SIZE
HEAD_DIM = 64
D_A = 384
D_B = 256
D_C = 384
N_C_HEADS = 6
POOL_WINDOWS = (2, 4, 8, 16)
POOL_GROUP = 64
POOL_BUF = 15
CONV_WIDTH = 4
RG_C = 8.0
D_IN = 2 * D_A + D_B + 3 * D_C
D_FF = 2816
D_PLE = 256
EPS = 1e-6
SCALE = HEAD_DIM ** -0.5
LOG2E = 1.4426950408889634

NP = BATCH * SEQ
NS = DEC_BATCH * DEC_SEQ
NT = NP + NS
TM = 512
NPT = NP // TM
NTT = NT // TM
TT = 512
TQ = 512
TK = 256
LANE = 128
HEAD_PAIR = 2 * HEAD_DIM
N_PAIRS = D_C // HEAD_PAIR
VMEM_LIMIT = 56 * 1024 * 1024


def _rms(x, g):
    ms = jnp.mean(x * x, axis=-1, keepdims=True)
    return x * lax.rsqrt(ms + EPS) * g


def _dot(a, b):
    return lax.dot_general(a, b, (((1,), (0,)), ((), ())), preferred_element_type=F32)


def _dot_nt(a, b):
    return lax.dot_general(a, b, (((1,), (1,)), ((), ())), preferred_element_type=F32)


def _softplus(z):
    return jnp.maximum(z, 0.0) + jnp.log(1.0 + jnp.exp2(jnp.abs(z) * (-LOG2E)))


def _hi_lo(x):
    hi = x.astype(BF16).astype(F32)
    return jnp.concatenate([hi, x - hi], axis=1)


def _suffix_ones(n):
    r = lax.broadcasted_iota(jnp.int32, (2 * n, n), 0) & (n - 1)
    c = lax.broadcasted_iota(jnp.int32, (2 * n, n), 1)
    return jnp.where(r >= c, 1.0, 0.0).astype(BF16)


def _layer_spec(shape, layer):
    nd = len(shape)
    return pl.BlockSpec((None,) + tuple(shape), lambda *_: (layer,) + (0,) * nd,
                        pipeline_mode=pl.Buffered(1))


def _params(*sem):
    return pltpu.CompilerParams(dimension_semantics=sem, vmem_limit_bytes=VMEM_LIMIT)


S_XB = 2 * D_A
S_Q = S_XB + D_B
S_K = S_Q + D_C
S_V = S_K + D_C


def _in_proj_kernel(h_ref, g_ref, w_ref, wkvt_ref, kt_prev, vt_prev,
                    xag_ref, xb_ref, qb_ref, vb_ref, ktb_ref, kt_ref, vt_ref, qkvs_ref, kts_ref, vts_ref):
    del kt_prev, vt_prev
    i = pl.program_id(0)
    hn = _rms(h_ref[...], g_ref[...]).astype(BF16)
    u = _dot(hn, w_ref[:, :S_K])
    v = _dot(hn, w_ref[:, S_V:])
    kvt = _dot_nt(wkvt_ref[...], hn)
    xag_ref[...] = u[:, :S_XB]
    xb_ref[...] = u[:, S_XB:S_Q]

    @pl.when(i < NPT)
    def _():
        qb_ref[...] = u[:, S_Q:].astype(BF16)
        vb_ref[...] = v.astype(BF16)
        kt = kvt[:D_C]
        kt_ref[...] = kt
        ktb_ref[...] = kt.astype(BF16)
        vt_ref[...] = kvt[D_C:]

    @pl.when(i == NPT)
    def _():
        qkvs_ref[:, :D_C] = u[:, S_Q:]
        qkvs_ref[:, D_C:2 * D_C] = _dot(hn, w_ref[:, S_K:S_V])
        qkvs_ref[:, 2 * D_C:] = v
        for t in range(DEC_SEQ):
            kts_ref[t] = kvt[:D_C, t * DEC_BATCH:(t + 1) * DEC_BATCH]
            vts_ref[t] = kvt[D_C:, t * DEC_BATCH:(t + 1) * DEC_BATCH]


def _in_proj(h, norm_mix, w_in_b, w_kvt_b, kt_all, vt_all, layer):
    tiles_per_seq = SEQ // TM
    pi = lambda i: jnp.minimum(i, NPT - 1)
    row = lambda w: pl.BlockSpec((TM, w), lambda i: (i, 0))
    prow = lambda w: pl.BlockSpec((TM, w), lambda i: (pi(i), 0))
    full = lambda *s: pl.BlockSpec(s, lambda i: (0,) * len(s))
    kv_t = pl.BlockSpec((None, None, D_C, TM), lambda i: (layer, pi(i) // tiles_per_seq, 0, pi(i) % tiles_per_seq))
    ktb = pl.BlockSpec((None, D_C, TM), lambda i: (pi(i) // tiles_per_seq, 0, pi(i) % tiles_per_seq))
    any_spec = pl.BlockSpec(memory_space=pl.ANY)
    kv_shape = jax.ShapeDtypeStruct((DEPTH, BATCH, D_C, SEQ), F32)
    return pl.pallas_call(
        _in_proj_kernel,
        grid=(NTT,),
        in_specs=[row(D_MODEL), _layer_spec((1, D_MODEL), layer), _layer_spec((D_MODEL, D_IN), layer),
                  _layer_spec((2 * D_C, D_MODEL), layer), any_spec, any_spec],
        out_specs=[row(2 * D_A), row(D_B), prow(D_C), prow(D_C), ktb, kv_t, kv_t,
                   full(NS, 3 * D_C), full(DEC_SEQ, D_C, DEC_BATCH), full(DEC_SEQ, D_C, DEC_BATCH)],
        out_shape=[jax.ShapeDtypeStruct((NT, 2 * D_A), F32), jax.ShapeDtypeStruct((NT, D_B), F32),
                   jax.ShapeDtypeStruct((NP, D_C), BF16), jax.ShapeDtypeStruct((NP, D_C), BF16),
                   jax.ShapeDtypeStruct((BATCH, D_C, SEQ), BF16), kv_shape, kv_shape,
                   jax.ShapeDtypeStruct((NS, 3 * D_C), F32),
                   jax.ShapeDtypeStruct((DEC_SEQ, D_C, DEC_BATCH), F32),
                   jax.ShapeDtypeStruct((DEC_SEQ, D_C, DEC_BATCH), F32)],
        input_output_aliases={4: 5, 5: 6},
        compiler_params=_params("arbitrary"),
        name="in_proj",
    )(h, norm_mix, w_in_b, w_kvt_b, kt_all, vt_all)


def _rg_gates(xc, wa, ba, wx, bx, lam):
    xcb = xc.astype(BF16)
    r = jax.nn.sigmoid(_dot(xcb, wa) + ba)
    ig = jax.nn.sigmoid(_dot(xcb, wx) + bx)
    log_a = RG_C * r * jax.nn.log_sigmoid(lam)
    a = jnp.exp(log_a)
    om = 1.0 - a * a
    u = jnp.where(om > 0.0, om * lax.rsqrt(om), 0.0) * (ig * xc)
    return a, u


def _pool_lane_select(lane, per_window):
    out = per_window[-1]
    for g in range(len(POOL_WINDOWS) - 2, -1, -1):
        out = jnp.where(lane < (g + 1) * POOL_GROUP, per_window[g], out)
    return out


def _mix_prompt_kernel(xag_ref, xb_ref, cw_ref, cb_ref, wa_ref, ba_ref, wx_ref, bx_ref, lam_ref,
                       na_ref, pw_ref, ps_ref,
                       ya_ref, yb_ref, convn_ref, rgn_ref, pooln_ref,
                       xbuf, pbuf, hcar, hs_s):
    j = pl.program_id(1)
    hist_a = 8
    hist_b = 16

    @pl.when(j == 0)
    def _():
        xbuf[0:hist_a, :] = jnp.zeros((hist_a, D_A), F32)
        pbuf[0:hist_b, :] = jnp.zeros((hist_b, D_B), F32)
        hcar[...] = jnp.zeros_like(hcar)

    xa = xag_ref[:, 0:D_A]
    ga = xag_ref[:, D_A:2 * D_A]
    xbuf[hist_a:hist_a + TT, :] = xa
    cw = cw_ref[...]
    assert CONV_WIDTH == 4
    e = xbuf[...]
    e1 = pltpu.roll(e, 1, 0)
    p2 = pltpu.roll(cw[1:2, :] * e + cw[0:1, :] * e1, 2, 0)
    xc = (cb_ref[...] + cw[3:4, :] * e + cw[2:3, :] * e1 + p2)[hist_a:hist_a + TT]
    convn_ref[0] = xbuf[hist_a + TT - (CONV_WIDTH - 1):hist_a + TT, :]
    xbuf[0:hist_a, :] = xbuf[TT:TT + hist_a, :]

    a, u = _rg_gates(xc, wa_ref[...], ba_ref[...], wx_ref[...], bx_ref[...], lam_ref[...])
    sub_rows = 8
    a3 = a.reshape(TT // sub_rows, sub_rows, D_A)
    u3 = u.reshape(TT // sub_rows, sub_rows, D_A)
    sub = lax.broadcasted_iota(jnp.int32, a3.shape, 1)
    d = 1
    while d < sub_rows:
        m = sub >= d
        a_s = jnp.where(m, pltpu.roll(a3, d, 1), 1.0)
        u_s = jnp.where(m, pltpu.roll(u3, d, 1), 0.0)
        u3 = a3 * u_s + u3
        a3 = a3 * a_s
        d *= 2
    h = hcar[0:1, :]
    for g in range(TT // sub_rows):
        hs_g = u3[g] + a3[g] * h
        hs_s[g * sub_rows:(g + 1) * sub_rows, :] = hs_g
        h = hs_g[sub_rows - 1:sub_rows, :]
    hcar[0:1, :] = h
    rgn_ref[0] = h
    hs = hs_s[...]
    ya_ref[...] = _rms(jax.nn.gelu(ga) * hs, na_ref[...])

    xb = xb_ref[...]
    pbuf[hist_b:hist_b + TT, :] = xb
    lane = lax.broadcasted_iota(jnp.int32, (TT, D_B), 1)
    pos = j * TT + lax.broadcasted_iota(jnp.int32, (TT, D_B), 0)
    assert POOL_WINDOWS == (2, 4, 8, 16) and D_B == 2 * LANE and 2 * POOL_GROUP == LANE
    e = pbuf[...]
    s2 = e + pltpu.roll(e, 1, 0)
    s4 = s2 + pltpu.roll(s2, 2, 0)
    s4_hi = s4[:, LANE:]
    s8_hi = s4_hi + pltpu.roll(s4_hi, 4, 0)
    s16_hi = s8_hi + pltpu.roll(s8_hi, 8, 0)
    tile = lambda x: x[hist_b:hist_b + TT]
    lane_h = lax.broadcasted_iota(jnp.int32, (TT, LANE), 1)
    wsum = jnp.concatenate(
        [jnp.where(lane_h < POOL_GROUP, tile(s2[:, :LANE]), tile(s4[:, :LANE])),
         jnp.where(lane_h < POOL_GROUP, tile(s8_hi), tile(s16_hi))], axis=1)
    win = _pool_lane_select(lane, [jnp.full((TT, D_B), w, jnp.int32) for w in POOL_WINDOWS])
    cnt = jnp.minimum(win, pos + 1).astype(F32)
    pooled = (wsum / cnt - xb).astype(BF16)
    yb_ref[...] = _dot(pooled, pw_ref[...]) * ps_ref[...]
    pooln_ref[0] = pbuf[hist_b + TT - POOL_BUF:hist_b + TT, :]
    pbuf[0:hist_b, :] = pbuf[TT:TT + hist_b, :]


def _mix_prompt(xag, xb, lw, layer):
    nj = SEQ // TT
    row = lambda w: pl.BlockSpec((TT, w), lambda b, j: (b * nj + j, 0))
    vec = lambda w: _layer_spec((1, w), layer)
    state = lambda r, w: pl.BlockSpec((1, r, w), lambda b, j: (b, 0, 0))
    return pl.pallas_call(
        _mix_prompt_kernel,
        grid=(BATCH, nj),
        in_specs=[row(2 * D_A), row(D_B), _layer_spec((CONV_WIDTH, D_A), layer), vec(D_A),
                  _layer_spec((D_A, D_A), layer), vec(D_A), _layer_spec((D_A, D_A), layer), vec(D_A),
                  vec(D_A), vec(D_A), _layer_spec((D_B, D_B), layer), vec(D_B)],
        out_specs=[row(D_A), row(D_B), state(CONV_WIDTH - 1, D_A), state(1, D_A), state(POOL_BUF, D_B)],
        out_shape=[jax.ShapeDtypeStruct((NP, D_A), F32), jax.ShapeDtypeStruct((NP, D_B), F32),
                   jax.ShapeDtypeStruct((BATCH, CONV_WIDTH - 1, D_A), F32),
                   jax.ShapeDtypeStruct((BATCH, 1, D_A), F32),
                   jax.ShapeDtypeStruct((BATCH, POOL_BUF, D_B), F32)],
        scratch_shapes=[pltpu.VMEM((TT + 8, D_A), F32), pltpu.VMEM((TT + 16, D_B), F32),
                        pltpu.VMEM((8, D_A), F32), pltpu.VMEM((TT, D_A), F32)],
        compiler_params=_params("parallel", "arbitrary"),
        name="mix_prompt",
    )(xag, xb, lw["conv_w"], lw["conv_b"], lw["rg_wa"], lw["rg_ba"], lw["rg_wx"], lw["rg_bx"],
      lw["rg_lambda"], lw["norm_a"], lw["pool_w"], lw["pool_scale"])


def _mix_sample_kernel(xag_ref, xb_ref, sconv_ref, srg_ref, spool_ref,
                       cw_ref, cb_ref, wa_ref, ba_ref, wx_ref, bx_ref, lam_ref, na_ref, pw_ref, ps_ref,
                       ya_ref, yb_ref, convn_ref, rgn_ref, pooln_ref):
    nb = DEC_BATCH
    rows = lambda t: slice(t * nb, (t + 1) * nb)
    xe = [sconv_ref[0, t] for t in range(CONV_WIDTH - 1)] + [xag_ref[rows(t), 0:D_A] for t in range(DEC_SEQ)]
    cw = cw_ref[...]
    xcs = []
    for t in range(DEC_SEQ):
        y = cb_ref[...]
        for jj in range(CONV_WIDTH):
            y = y + cw[jj:jj + 1, :] * xe[t + jj]
        xcs.append(y)
    for t in range(CONV_WIDTH - 1):
        convn_ref[t] = xe[len(xe) - (CONV_WIDTH - 1) + t]
    xc = jnp.concatenate(xcs, axis=0)
    a, u = _rg_gates(xc, wa_ref[...], ba_ref[...], wx_ref[...], bx_ref[...], lam_ref[...])
    h = srg_ref[0]
    hs = []
    for t in range(DEC_SEQ):
        h = a[rows(t)] * h + u[rows(t)]
        hs.append(h)
    rgn_ref[...] = h
    ga = xag_ref[:, D_A:2 * D_A]
    ya_ref[...] = _rms(jax.nn.gelu(ga) * jnp.concatenate(hs, axis=0), na_ref[...])

    pe = [spool_ref[0, t] for t in range(POOL_BUF)] + [xb_ref[rows(t), :] for t in range(DEC_SEQ)]
    for t in range(POOL_BUF):
        pooln_ref[t] = pe[len(pe) - POOL_BUF + t]
    lane = lax.broadcasted_iota(jnp.int32, (nb, D_B), 1)
    pooled = []
    for t in range(DEC_SEQ):
        s = pe[POOL_BUF + t]
        means = []
        for lag in range(1, max(POOL_WINDOWS)):
            s = s + pe[POOL_BUF + t - lag]
            if lag + 1 in POOL_WINDOWS:
                means.append(s / float(min(lag + 1, PAST_LEN + t + 1)))
        pooled.append(_pool_lane_select(lane, means) - pe[POOL_BUF + t])
    pooled = jnp.concatenate(pooled, axis=0).astype(BF16)
    yb_ref[...] = _dot(pooled, pw_ref[...]) * ps_ref[...]


def _mix_sample(xag, xb, sconv_t, srg, spool_t, lw, layer):
    vec = lambda w: _layer_spec((1, w), layer)
    lay = lambda *s: pl.BlockSpec((1,) + s, lambda i: (layer,) + (0,) * len(s))
    full = lambda *s: pl.BlockSpec(s, lambda i: (0,) * len(s))
    return pl.pallas_call(
        _mix_sample_kernel,
        grid=(1,),
        in_specs=[pl.BlockSpec((NS, 2 * D_A), lambda i: (NPT, 0)), pl.BlockSpec((NS, D_B), lambda i: (NPT, 0)),
                  lay(CONV_WIDTH - 1, DEC_BATCH, D_A), lay(DEC_BATCH, D_A), lay(POOL_BUF, DEC_BATCH, D_B),
                  _layer_spec((CONV_WIDTH, D_A), layer), vec(D_A),
                  _layer_spec((D_A, D_A), layer), vec(D_A), _layer_spec((D_A, D_A), layer), vec(D_A),
                  vec(D_A), vec(D_A), _layer_spec((D_B, D_B), layer), vec(D_B)],
        out_specs=[full(NS, D_A), full(NS, D_B), full(CONV_WIDTH - 1, DEC_BATCH, D_A), full(DEC_BATCH, D_A),
                   full(POOL_BUF, DEC_BATCH, D_B)],
        out_shape=[jax.ShapeDtypeStruct((NS, D_A), F32), jax.ShapeDtypeStruct((NS, D_B), F32),
                   jax.ShapeDtypeStruct((CONV_WIDTH - 1, DEC_BATCH, D_A), F32),
                   jax.ShapeDtypeStruct((DEC_BATCH, D_A), F32),
                   jax.ShapeDtypeStruct((POOL_BUF, DEC_BATCH, D_B), F32)],
        compiler_params=_params("arbitrary"),
        name="mix_sample",
    )(xag, xb, sconv_t, srg, spool_t, lw["conv_w"], lw["conv_b"], lw["rg_wa"], lw["rg_ba"], lw["rg_wx"],
      lw["rg_bx"], lw["rg_lambda"], lw["norm_a"], lw["pool_w"], lw["pool_scale"])


def _attn_prompt_kernel(bias_ref, q_ref, kt_ref, v_ref, o_ref, za_s, zb_s, acc_s, run_s):
    assert TQ == 2 * TK
    p = pl.program_id(1)
    i = pl.program_id(2)
    q = q_ref[...]
    lane_q = lax.broadcasted_iota(jnp.int32, (TQ, HEAD_PAIR), 1)
    qs = jnp.concatenate([jnp.where(lane_q < HEAD_DIM, q, jnp.zeros_like(q)),
                          jnp.where(lane_q >= HEAD_DIM, q, jnp.zeros_like(q))], axis=0) * jnp.asarray(SCALE, BF16)
    r = lax.broadcasted_iota(jnp.int32, (2 * TQ, TK), 0) & (TQ - 1)
    c = lax.broadcasted_iota(jnp.int32, (2 * TQ, TK), 1)
    ones = _suffix_ones(TK)
    bias0 = bias_ref[2 * p]
    bias1 = bias_ref[2 * p + 1]

    def scores(j):
        start = pl.multiple_of(j * TK, TK)
        z = _dot(qs, kt_ref[:, pl.ds(start, TK)])
        return jnp.concatenate([z[:TQ] + bias0, z[TQ:] + bias1], axis=0)

    def process(z_ref, j, diag_off):
        z = z_ref[...]
        start = pl.multiple_of(j * TK, TK)
        sp = _softplus(z)
        if diag_off is not None:
            causal = c + diag_off * TK < r
            sp = jnp.where(causal, sp, 0.0)
        run = run_s[...]
        cs = _dot(_hi_lo(sp), ones)
        a = jnp.exp(z - (cs + jnp.concatenate([run] * (TK // LANE), axis=1)))
        if diag_off is not None:
            a = jnp.where(causal, a, 0.0)
        acc_s[...] += _dot(a, v_ref[pl.ds(start, TK), :])
        run_s[...] = run + jnp.broadcast_to(cs[:, 0:1], run.shape)

    acc_s[...] = jnp.zeros_like(acc_s)
    run_s[...] = jnp.zeros_like(run_s)
    t0 = 2 * i + 1
    za_s[...] = scores(t0)
    zb_s[...] = scores(t0 - 1)
    process(za_s, t0, 1)
    za_s[...] = scores(jnp.maximum(t0 - 2, 0))
    process(zb_s, t0 - 1, 0)

    def two_tiles(jj, carry):
        t = t0 - 2 - 2 * jj
        zb_s[...] = scores(t - 1)
        process(za_s, t, None)
        za_s[...] = scores(jnp.maximum(t - 2, 0))
        process(zb_s, t - 1, None)
        return carry

    lax.fori_loop(0, i, two_tiles, 0)
    acc = acc_s[...]
    o_ref[...] = jnp.where(lane_q < HEAD_DIM, acc[:TQ], acc[TQ:])


def _attn_prompt(bias, qb, ktb, vb):
    nq = SEQ // TQ
    return pl.pallas_call(
        _attn_prompt_kernel,
        grid=(BATCH, N_PAIRS, nq),
        in_specs=[pl.BlockSpec(memory_space=pltpu.SMEM),
                  pl.BlockSpec((TQ, HEAD_PAIR), lambda b, p, i: (b * nq + i, p)),
                  pl.BlockSpec((None, HEAD_PAIR, SEQ), lambda b, p, i: (b, p, 0)),
                  pl.BlockSpec((SEQ, HEAD_PAIR), lambda b, p, i: (b, p))],
        out_specs=pl.BlockSpec((TQ, HEAD_PAIR), lambda b, p, i: (b * nq + i, p)),
        out_shape=jax.ShapeDtypeStruct((NP, D_C), F32),
        scratch_shapes=[pltpu.VMEM((2 * TQ, TK), F32), pltpu.VMEM((2 * TQ, TK), F32),
                        pltpu.VMEM((2 * TQ, HEAD_PAIR), F32), pltpu.VMEM((2 * TQ, HEAD_PAIR), F32)],
        compiler_params=_params("parallel", "parallel", "arbitrary"),
        name="attn_prompt",
    )(bias, qb, ktb, vb)


QROWS = 32
N_BLOCKS = N_PAGES + 1


def _attn_decode_kernel(pt_ref, bias_ref, q8_ref, kn_ref, vn_ref, *refs):
    del pt_ref
    ktp = refs[0:N_PAGES]
    vtp = refs[N_PAGES:2 * N_PAGES]
    o_ref = refs[2 * N_PAGES]
    kn_s, vn_s = refs[2 * N_PAGES + 1:]
    b = pl.program_id(0)

    @pl.when(b == 0)
    def _():
        kn_s[...] = jnp.zeros_like(kn_s)
        vn_s[...] = jnp.zeros_like(vn_s)

    kn_s[0:DEC_SEQ, :] = kn_ref[0]
    vn_s[0:DEC_SEQ, :] = vn_ref[0]

    rq = lax.broadcasted_iota(jnp.int32, (QROWS, D_C), 0)
    cq = lax.broadcasted_iota(jnp.int32, (QROWS, D_C), 1)
    head_lanes = (cq >> 6) == (rq >> 2)
    q8 = q8_ref[0] * SCALE
    qbd = jnp.where(head_lanes, jnp.concatenate([q8] * (QROWS // 8), axis=0), 0.0)

    rr = lax.broadcasted_iota(jnp.int32, (QROWS, PAGE_SIZE), 0)
    cc = lax.broadcasted_iota(jnp.int32, (QROWS, PAGE_SIZE), 1)
    bias = jnp.zeros((QROWS, PAGE_SIZE), F32)
    for h in range(N_C_HEADS):
        bias = jnp.where((rr >> 2) == h, bias_ref[h], bias)
    new_valid = cc < (rr & 3)

    zs = []
    for pp in range(N_PAGES // 2):
        kt2 = jnp.concatenate([ktp[2 * pp][...], ktp[2 * pp + 1][...]], axis=1)
        z2 = _dot(qbd, kt2)
        zs += [z2[:, :PAGE_SIZE] + bias, z2[:, PAGE_SIZE:] + bias]
    z_old = jnp.concatenate(zs, axis=0)
    z_new = _dot_nt(qbd, kn_s[...]) + bias
    sp = jnp.concatenate([_softplus(z_old), jnp.where(new_valid, _softplus(z_new), 0.0)], axis=0)
    z = jnp.concatenate([z_old, z_new], axis=0)
    cs = _dot(_hi_lo(sp), _suffix_ones(PAGE_SIZE))

    run = jnp.zeros((QROWS, 1), F32)
    runs = [None] * N_BLOCKS
    for n in range(N_BLOCKS - 1, -1, -1):
        runs[n] = jnp.broadcast_to(run, (QROWS, PAGE_SIZE))
        run = run + cs[n * QROWS:(n + 1) * QROWS, 0:1]
    a = jnp.exp(z - (cs + jnp.concatenate(runs, axis=0)))
    blk = lambda n: a[n * QROWS:(n + 1) * QROWS]

    acc = _dot(jnp.where(new_valid, blk(N_PAGES), 0.0), vn_s[...])
    for pp in range(N_PAGES // 2):
        a2 = jnp.concatenate([blk(2 * pp), blk(2 * pp + 1)], axis=1)
        vt2 = jnp.concatenate([vtp[2 * pp][...], vtp[2 * pp + 1][...]], axis=1)
        acc = acc + _dot_nt(a2, vt2)

    acc = jnp.where(head_lanes, acc, 0.0)
    s = acc[0:8] + acc[8:16] + acc[16:24]
    s = s + pltpu.roll(s, 4, 0)
    o_ref[0] = s[0:DEC_SEQ]


def _attn_decode(page_table_flat, bias, q8, kn, vn, cache_kt, cache_vt, layer):
    def page_spec(pg):
        return pl.BlockSpec((None, None, D_C, PAGE_SIZE),
                            lambda b, pt: (layer, pt[b * N_PAGES + pg], 0, 0))

    seq = lambda r: pl.BlockSpec((1, r, D_C), lambda b, pt: (b, 0, 0))
    grid_spec = pltpu.PrefetchScalarGridSpec(
        num_scalar_prefetch=1,
        grid=(DEC_BATCH,),
        in_specs=[pl.BlockSpec(memory_space=pltpu.SMEM), seq(8), seq(DEC_SEQ), seq(DEC_SEQ)]
        + [page_spec(pg) for pg in range(N_PAGES)] * 2,
        out_specs=seq(DEC_SEQ),
        scratch_shapes=[pltpu.VMEM((PAGE_SIZE, D_C), F32), pltpu.VMEM((PAGE_SIZE, D_C), F32)],
    )
    return pl.pallas_call(
        _attn_decode_kernel,
        grid_spec=grid_spec,
        out_shape=jax.ShapeDtypeStruct((DEC_BATCH, DEC_SEQ, D_C), F32),
        compiler_params=_params("arbitrary"),
        name="attn_decode",
    )(page_table_flat, bias, q8, kn, vn, *([cache_kt] * N_PAGES), *([cache_vt] * N_PAGES))


MXU_TILE = 256
FF_SPLIT = (D_FF // MXU_TILE + 1) // 2 * MXU_TILE


def _tail_kernel(last, yap_ref, ybp_ref, op_ref, pep_ref, yas_ref, ybs_ref, os_ref, pes_ref, h_ref,
                 nc_ref, wout_ref, nf_ref, wg_ref, wu_ref, wd_ref, np_ref, wpg_ref, wpp_ref, nfin_ref, *out_refs):
    i = pl.program_id(0)
    is_sample = i == NPT
    pick = lambda s_ref, p_ref: jnp.where(is_sample, s_ref[...], p_ref[...])
    yc = _rms(pick(os_ref, op_ref), nc_ref[...])
    y = jnp.concatenate([pick(yas_ref, yap_ref), pick(ybs_ref, ybp_ref), yc], axis=-1).astype(BF16)
    h = h_ref[...] + _dot(y, wout_ref[...])

    hn = _rms(h, nf_ref[...]).astype(BF16)
    for cols in (slice(0, FF_SPLIT), slice(FF_SPLIT, D_FF)):
        act = (jax.nn.silu(_dot(hn, wg_ref[:, cols])) * _dot(hn, wu_ref[:, cols])).astype(BF16)
        h = h + _dot(act, wd_ref[cols, :])

    hn = _rms(h, np_ref[...]).astype(BF16)
    gate = jax.nn.sigmoid(_dot(hn, wpg_ref[...]))
    h = h + gate * _dot(pick(pes_ref, pep_ref).astype(BF16), wpp_ref[...])

    if not last:
        out_refs[0][...] = h
    else:
        y_out = _rms(h, nfin_ref[...])
        yp_ref, ys_ref = out_refs

        @pl.when(i < NPT)
        def _():
            yp_ref[...] = y_out

        @pl.when(is_sample)
        def _():
            ys_ref[...] = y_out


def _tail(ya_p, yb_p, o_p, pe_p, ya_s, yb_s, o_s, pe_s, h, w, norm_final, layer):
    last = layer == DEPTH - 1
    pi = lambda i: jnp.minimum(i, NPT - 1)
    prow = lambda wd: pl.BlockSpec((TM, wd), lambda i: (pi(i), 0))
    srow = lambda wd: pl.BlockSpec((NS, wd), lambda i: (0, 0), pipeline_mode=pl.Buffered(1))
    row = lambda wd: pl.BlockSpec((TM, wd), lambda i: (i, 0))
    vec = lambda wd: _layer_spec((1, wd), layer)
    if last:
        out_specs = [prow(D_MODEL), pl.BlockSpec((NS, D_MODEL), lambda i: (0, 0))]
        out_shape = [jax.ShapeDtypeStruct((NP, D_MODEL), F32), jax.ShapeDtypeStruct((NS, D_MODEL), F32)]
    else:
        out_specs = [row(D_MODEL)]
        out_shape = [jax.ShapeDtypeStruct((NT, D_MODEL), F32)]
    return pl.pallas_call(
        functools.partial(_tail_kernel, last),
        grid=(NTT,),
        in_specs=[prow(D_A), prow(D_B), prow(D_C),
                  pl.BlockSpec((None, TM, D_PLE), lambda i: (layer, pi(i), 0)),
                  srow(D_A), srow(D_B), srow(D_C),
                  pl.BlockSpec((None, NS, D_PLE), lambda i: (layer, 0, 0), pipeline_mode=pl.Buffered(1)),
                  row(D_MODEL),
                  vec(D_C), _layer_spec((D_MODEL, D_MODEL), layer),
                  vec(D_MODEL), _layer_spec((D_MODEL, D_FF), layer), _layer_spec((D_MODEL, D_FF), layer),
                  _layer_spec((D_FF, D_MODEL), layer),
                  vec(D_MODEL), _layer_spec((D_MODEL, D_MODEL), layer), _layer_spec((D_PLE, D_MODEL), layer),
                  pl.BlockSpec((1, D_MODEL), lambda i: (0, 0), pipeline_mode=pl.Buffered(1))],
        out_specs=out_specs,
        out_shape=out_shape,
        compiler_params=_params("arbitrary"),
        name="tail",
    )(ya_p, yb_p, o_p, pe_p, ya_s, yb_s, o_s, pe_s, h,
      w["norm_c"], w["w_out"], w["norm_ffn"], w["w_ff_gate"], w["w_ff_up"], w["w_ff_down"],
      w["norm_ple"], w["w_ple_gate"], w["w_ple_proj"], norm_final)


def _block_diag(w):
    depth, g, n, _ = w.shape
    eye = jnp.eye(g, dtype=w.dtype)
    return jnp.einsum("dgij,gh->dgihj", w, eye).reshape(depth, g * n, g * n)


def kernel(x_prompt, x_sample, cache_k, cache_v, state_conv, state_rglru, state_pool, page_table,
           p_prompt, p_sample, norm_mix, w_in, conv_w, conv_b, rg_wa, rg_ba, rg_wx, rg_bx, rg_lambda,
           norm_a, pool_w, pool_scale, norm_c, sb_bias, w_out, norm_ffn, w_ff_gate, w_ff_up, w_ff_down,
           norm_ple, w_ple_gate, w_ple_proj, norm_final):
    n_pool = cache_k.shape[1]
    tmajor = lambda x: jnp.swapaxes(x, -3, -2)

    h = jnp.concatenate([x_prompt.reshape(NP, D_MODEL), tmajor(x_sample).reshape(NS, D_MODEL)], axis=0)
    pe_p = p_prompt.reshape(DEPTH, NP, D_PLE)
    pe_s = tmajor(p_sample).reshape(DEPTH, NS, D_PLE)
    ckt = jnp.transpose(cache_k, (0, 1, 3, 4, 2)).reshape(DEPTH, n_pool, D_C, PAGE_SIZE)
    cvt = jnp.transpose(cache_v, (0, 1, 3, 4, 2)).reshape(DEPTH, n_pool, D_C, PAGE_SIZE)
    pt = page_table.reshape(-1)
    sconv_t = tmajor(state_conv)
    spool_t = tmajor(state_pool)

    vec = lambda x: x.reshape(DEPTH, 1, -1)
    lw = {
        "conv_w": conv_w, "conv_b": vec(conv_b),
        "rg_wa": _block_diag(rg_wa).astype(BF16), "rg_ba": vec(rg_ba),
        "rg_wx": _block_diag(rg_wx).astype(BF16), "rg_bx": vec(rg_bx),
        "rg_lambda": vec(rg_lambda), "norm_a": vec(norm_a),
        "pool_w": _block_diag(pool_w).astype(BF16), "pool_scale": vec(pool_scale),
    }
    w_in_b = w_in.astype(BF16)
    w_kvt_b = jnp.swapaxes(w_in[:, :, S_K:], 1, 2).astype(BF16)
    norm_mix3 = vec(norm_mix)
    tw = {
        "norm_c": vec(norm_c), "w_out": w_out.astype(BF16), "norm_ffn": vec(norm_ffn),
        "w_ff_gate": w_ff_gate.astype(BF16), "w_ff_up": w_ff_up.astype(BF16), "w_ff_down": w_ff_down.astype(BF16),
        "norm_ple": vec(norm_ple), "w_ple_gate": w_ple_gate.astype(BF16), "w_ple_proj": w_ple_proj.astype(BF16),
    }
    norm_final2 = norm_final.reshape(1, D_MODEL)

    kt_all = jnp.zeros((DEPTH, BATCH, D_C, SEQ), F32)
    vt_all = jnp.zeros((DEPTH, BATCH, D_C, SEQ), F32)
    outs = {n: [] for n in ("ks", "vs", "cp", "cs", "rp", "rs", "pp", "ps")}
    for l in range(DEPTH):
        xag, xb, qb, vb, ktb, kt_all, vt_all, qkv_s, kt_s, vt_s = _in_proj(
            h, norm_mix3, w_in_b, w_kvt_b, kt_all, vt_all, l)
        ya_p, yb_p, conv_p, rg_p, pool_p = _mix_prompt(xag, xb, lw, l)
        ya_s, yb_s, conv_s, rg_s, pool_s = _mix_sample(xag, xb, sconv_t, state_rglru, spool_t, lw, l)
        o_p = _attn_prompt(sb_bias[l], qb, ktb, vb)

        bmajor = lambda x: tmajor(x.reshape(DEC_SEQ, DEC_BATCH, D_C))
        q_s = bmajor(qkv_s[:, :D_C])
        o_s = _attn_decode(pt, sb_bias[l], jnp.concatenate([q_s, q_s], axis=1),
                           bmajor(qkv_s[:, D_C:2 * D_C]), bmajor(qkv_s[:, 2 * D_C:]), ckt, cvt, l)
        o_s = tmajor(o_s).reshape(NS, D_C)

        res = _tail(ya_p, yb_p, o_p, pe_p, ya_s, yb_s, o_s, pe_s, h, tw, norm_final2, l)
        h = res[0]

        outs["ks"].append(kt_s)
        outs["vs"].append(vt_s)
        outs["cp"].append(conv_p)
        outs["cs"].append(conv_s)
        outs["rp"].append(rg_p.reshape(BATCH, D_A))
        outs["rs"].append(rg_s)
        outs["pp"].append(pool_p)
        outs["ps"].append(pool_s)

    y_prompt = res[0].reshape(BATCH, SEQ, D_MODEL)
    y_sample = tmajor(res[1].reshape(DEC_SEQ, DEC_BATCH, D_MODEL))
    st = lambda n: jnp.stack(outs[n])
    kv_prompt = lambda x: jnp.transpose(x.reshape(DEPTH, BATCH, N_C_HEADS, HEAD_DIM, SEQ), (0, 1, 4, 2, 3))
    kv_sample = lambda x: jnp.transpose(x.reshape(DEPTH, DEC_SEQ, N_C_HEADS, HEAD_DIM, DEC_BATCH), (0, 4, 1, 2, 3))
    return (y_prompt, y_sample, kv_prompt(kt_all), kv_prompt(vt_all), kv_sample(st("ks")), kv_sample(st("vs")),
            st("cp"), tmajor(st("cs")), st("rp"), st("rs"), st("pp"), tmajor(st("ps")))
```

```python
import functools

import jax
import jax.numpy as jnp
from jax import lax
from jax.experimental import pallas as pl
from jax.experimental.pallas import tpu as pltpu

F32 = jnp.float32
BF16 = jnp.bfloat16

D_MODEL = 1024
BATCH = 4
SEQ = 4096
DEPTH = 4
DEC_BATCH = 128
DEC_SEQ = 4
PAST_LEN = 2048
PAGE_SIZE = 128
N_PAGES = PAST_LEN // PAGE_SIZE
HEAD_DIM = 64
D_A = 384
D_B = 256
D_C = 384
N_C_HEADS = 6
POOL_WINDOWS = (2, 4, 8, 16)
POOL_GROUP = 64
POOL_BUF = 15
CONV_WIDTH = 4
RG_C = 8.0
D_IN = 2 * D_A + D_B + 3 * D_C
D_FF = 2816
D_PLE = 256
EPS = 1e-6
SCALE = HEAD_DIM ** -0.5
LOG2E = 1.4426950408889634

NP = BATCH * SEQ
NS = DEC_BATCH * DEC_SEQ
NT = NP + NS
TM = 512
NPT = NP // TM
NTT = NT // TM
TT = 512
TQ = 512
TK = 256
LANE = 128
HEAD_PAIR = 2 * HEAD_DIM
N_PAIRS = D_C // HEAD_PAIR
VMEM_LIMIT = 56 * 1024 * 1024


def _rms(x, g):
    ms = jnp.mean(x * x, axis=-1, keepdims=True)
    return x * lax.rsqrt(ms + EPS) * g


def _dot(a, b):
    return lax.dot_general(a, b, (((1,), (0,)), ((), ())), preferred_element_type=F32)


def _dot_nt(a, b):
    return lax.dot_general(a, b, (((1,), (1,)), ((), ())), preferred_element_type=F32)


def _softplus(z):
    return jnp.maximum(z, 0.0) + jnp.log(1.0 + jnp.exp2(jnp.abs(z) * (-LOG2E)))


def _hi_lo(x):
    hi = x.astype(BF16).astype(F32)
    return jnp.concatenate([hi, x - hi], axis=1)


def _suffix_ones(n):
    r = lax.broadcasted_iota(jnp.int32, (2 * n, n), 0) & (n - 1)
    c = lax.broadcasted_iota(jnp.int32, (2 * n, n), 1)
    return jnp.where(r >= c, 1.0, 0.0).astype(BF16)


def _layer_spec(shape, layer):
    nd = len(shape)
    return pl.BlockSpec((None,) + tuple(shape), lambda *_: (layer,) + (0,) * nd,
                        pipeline_mode=pl.Buffered(1))


def _params(*sem):
    return pltpu.CompilerParams(dimension_semantics=sem, vmem_limit_bytes=VMEM_LIMIT)


S_XB = 2 * D_A
S_Q = S_XB + D_B
S_K = S_Q + D_C
S_V = S_K + D_C


def _load_h(h_refs):
    if len(h_refs) == 1:
        return h_refs[0][...]
    return jnp.where(pl.program_id(0) == NPT, h_refs[1][...], h_refs[0][...])


def _h_specs(split):
    if not split:
        return [pl.BlockSpec((TM, D_MODEL), lambda i: (i, 0))]
    return [pl.BlockSpec((TM, D_MODEL), lambda i: (jnp.minimum(i, NPT - 1), 0)),
            pl.BlockSpec((NS, D_MODEL), lambda i: (0, 0), pipeline_mode=pl.Buffered(1))]


def _in_proj_kernel(n_h, *refs):
    h_refs = refs[:n_h]
    (g_ref, w_ref, wkvt_ref, kt_prev, vt_prev,
     xag_ref, xb_ref, qb_ref, vb_ref, ktb_ref, kt_ref, vt_ref, qkvs_ref, kts_ref, vts_ref) = refs[n_h:]
    del kt_prev, vt_prev
    i = pl.program_id(0)
    hn = _rms(_load_h(h_refs), g_ref[...]).astype(BF16)
    u = _dot(hn, w_ref[:, :S_K])
    v = _dot(hn, w_ref[:, S_V:])
    kvt = _dot_nt(wkvt_ref[...], hn)
    xag_ref[...] = u[:, :S_XB]
    xb_ref[...] = u[:, S_XB:S_Q]

    @pl.when(i < NPT)
    def _():
        qb_ref[...] = u[:, S_Q:].astype(BF16)
        vb_ref[...] = v.astype(BF16)
        kt = kvt[:D_C]
        kt_ref[...] = kt
        ktb_ref[...] = kt.astype(BF16)
        vt_ref[...] = kvt[D_C:]

    @pl.when(i == NPT)
    def _():
        qkvs_ref[:, :D_C] = u[:, S_Q:]
        qkvs_ref[:, D_C:2 * D_C] = _dot(hn, w_ref[:, S_K:S_V])
        qkvs_ref[:, 2 * D_C:] = v
        for t in range(DEC_SEQ):
            kts_ref[t] = kvt[:D_C, t * DEC_BATCH:(t + 1) * DEC_BATCH]
            vts_ref[t] = kvt[D_C:, t * DEC_BATCH:(t + 1) * DEC_BATCH]


def _in_proj(hs, norm_mix, w_in_b, w_kvt_b, kt_all, vt_all, layer):
    n_h = len(hs)
    tiles_per_seq = SEQ // TM
    pi = lambda i: jnp.minimum(i, NPT - 1)
    row = lambda w: pl.BlockSpec((TM, w), lambda i: (i, 0))
    prow = lambda w: pl.BlockSpec((TM, w), lambda i: (pi(i), 0))
    full = lambda *s: pl.BlockSpec(s, lambda i: (0,) * len(s))
    kv_t = pl.BlockSpec((None, None, D_C, TM), lambda i: (layer, pi(i) // tiles_per_seq, 0, pi(i) % tiles_per_seq))
    ktb = pl.BlockSpec((None, D_C, TM), lambda i: (pi(i) // tiles_per_seq, 0, pi(i) % tiles_per_seq))
    any_spec = pl.BlockSpec(memory_space=pl.ANY)
    kv_shape = jax.ShapeDtypeStruct((DEPTH, BATCH, D_C, SEQ), F32)
    return pl.pallas_call(
        functools.partial(_in_proj_kernel, n_h),
        grid=(NTT,),
        in_specs=_h_specs(n_h == 2) + [_layer_spec((1, D_MODEL), layer), _layer_spec((D_MODEL, D_IN), layer),
                                       _layer_spec((2 * D_C, D_MODEL), layer), any_spec, any_spec],
        out_specs=[row(2 * D_A), row(D_B), prow(D_C), prow(D_C), ktb, kv_t, kv_t,
                   full(NS, 3 * D_C), full(DEC_SEQ, D_C, DEC_BATCH), full(DEC_SEQ, D_C, DEC_BATCH)],
        out_shape=[jax.ShapeDtypeStruct((NT, 2 * D_A), F32), jax.ShapeDtypeStruct((NT, D_B), F32),
                   jax.ShapeDtypeStruct((NP, D_C), BF16), jax.ShapeDtypeStruct((NP, D_C), BF16),
                   jax.ShapeDtypeStruct((BATCH, D_C, SEQ), BF16), kv_shape, kv_shape,
                   jax.ShapeDtypeStruct((NS, 3 * D_C), F32),
                   jax.ShapeDtypeStruct((DEC_SEQ, D_C, DEC_BATCH), F32),
                   jax.ShapeDtypeStruct((DEC_SEQ, D_C, DEC_BATCH), F32)],
        input_output_aliases={n_h + 3: 5, n_h + 4: 6},
        compiler_params=_params("arbitrary"),
        name="in_proj",
    )(*hs, norm_mix, w_in_b, w_kvt_b, kt_all, vt_all)


def _rg_gates(xc, wa, ba, wx, bx, lam):
    xcb = xc.astype(BF16)
    r = jax.nn.sigmoid(_dot(xcb, wa) + ba)
    ig = jax.nn.sigmoid(_dot(xcb, wx) + bx)
    log_a = RG_C * r * jax.nn.log_sigmoid(lam)
    a = jnp.exp(log_a)
    om = 1.0 - a * a
    u = jnp.where(om > 0.0, om * lax.rsqrt(om), 0.0) * (ig * xc)
    return a, u


def _pool_lane_select(lane, per_window):
    out = per_window[-1]
    for g in range(len(POOL_WINDOWS) - 2, -1, -1):
        out = jnp.where(lane < (g + 1) * POOL_GROUP, per_window[g], out)
    return out


def _mix_prompt_kernel(xag_ref, xb_ref, cw_ref, cb_ref, wa_ref, ba_ref, wx_ref, bx_ref, lam_ref,
                       na_ref, pw_ref, ps_ref,
                       ya_ref, yb_ref, convn_ref, rgn_ref, pooln_ref,
                       xbuf, pbuf, hcar, hs_s):
    j = pl.program_id(1)
    hist_a = 8
    hist_b = 16

    @pl.when(j == 0)
    def _():
        xbuf[0:hist_a, :] = jnp.zeros((hist_a, D_A), F32)
        pbuf[0:hist_b, :] = jnp.zeros((hist_b, D_B), F32)
        hcar[...] = jnp.zeros_like(hcar)

    xa = xag_ref[:, 0:D_A]
    ga = xag_ref[:, D_A:2 * D_A]
    xbuf[hist_a:hist_a + TT, :] = xa
    cw = cw_ref[...]
    assert CONV_WIDTH == 4
    e = xbuf[...]
    e1 = pltpu.roll(e, 1, 0)
    p2 = pltpu.roll(cw[1:2, :] * e + cw[0:1, :] * e1, 2, 0)
    xc = (cb_ref[...] + cw[3:4, :] * e + cw[2:3, :] * e1 + p2)[hist_a:hist_a + TT]
    convn_ref[0] = xbuf[hist_a + TT - (CONV_WIDTH - 1):hist_a + TT, :]
    xbuf[0:hist_a, :] = xbuf[TT:TT + hist_a, :]

    a, u = _rg_gates(xc, wa_ref[...], ba_ref[...], wx_ref[...], bx_ref[...], lam_ref[...])
    sub_rows = 8
    a3 = a.reshape(TT // sub_rows, sub_rows, D_A)
    u3 = u.reshape(TT // sub_rows, sub_rows, D_A)
    sub = lax.broadcasted_iota(jnp.int32, a3.shape, 1)
    d = 1
    while d < sub_rows:
        m = sub >= d
        a_s = jnp.where(m, pltpu.roll(a3, d, 1), 1.0)
        u_s = jnp.where(m, pltpu.roll(u3, d, 1), 0.0)
        u3 = a3 * u_s + u3
        a3 = a3 * a_s
        d *= 2
    h = hcar[0:1, :]
    for g in range(TT // sub_rows):
        hs_g = u3[g] + a3[g] * h
        hs_s[g * sub_rows:(g + 1) * sub_rows, :] = hs_g
        h = hs_g[sub_rows - 1:sub_rows, :]
    hcar[0:1, :] = h
    rgn_ref[0] = h
    hs = hs_s[...]
    ya_ref[...] = _rms(jax.nn.gelu(ga) * hs, na_ref[...])

    xb = xb_ref[...]
    pbuf[hist_b:hist_b + TT, :] = xb
    lane = lax.broadcasted_iota(jnp.int32, (TT, D_B), 1)
    pos = j * TT + lax.broadcasted_iota(jnp.int32, (TT, D_B), 0)
    assert POOL_WINDOWS == (2, 4, 8, 16) and D_B == 2 * LANE and 2 * POOL_GROUP == LANE
    e = pbuf[...]
    s2 = e + pltpu.roll(e, 1, 0)
    s4 = s2 + pltpu.roll(s2, 2, 0)
    s4_hi = s4[:, LANE:]
    s8_hi = s4_hi + pltpu.roll(s4_hi, 4, 0)
    s16_hi = s8_hi + pltpu.roll(s8_hi, 8, 0)
    tile = lambda x: x[hist_b:hist_b + TT]
    lane_h = lax.broadcasted_iota(jnp.int32, (TT, LANE), 1)
    wsum = jnp.concatenate(
        [jnp.where(lane_h < POOL_GROUP, tile(s2[:, :LANE]), tile(s4[:, :LANE])),
         jnp.where(lane_h < POOL_GROUP, tile(s8_hi), tile(s16_hi))], axis=1)
    win = _pool_lane_select(lane, [jnp.full((TT, D_B), w, jnp.int32) for w in POOL_WINDOWS])
    cnt = jnp.minimum(win, pos + 1).astype(F32)
    pooled = (wsum / cnt - xb).astype(BF16)
    yb_ref[...] = _dot(pooled, pw_ref[...]) * ps_ref[...]
    pooln_ref[0] = pbuf[hist_b + TT - POOL_BUF:hist_b + TT, :]
    pbuf[0:hist_b, :] = pbuf[TT:TT + hist_b, :]


def _mix_prompt(xag, xb, lw, layer):
    nj = SEQ // TT
    row = lambda w: pl.BlockSpec((TT, w), lambda b, j: (b * nj + j, 0))
    vec = lambda w: _layer_spec((1, w), layer)
    state = lambda r, w: pl.BlockSpec((1, r, w), lambda b, j: (b, 0, 0))
    return pl.pallas_call(
        _mix_prompt_kernel,
        grid=(BATCH, nj),
        in_specs=[row(2 * D_A), row(D_B), _layer_spec((CONV_WIDTH, D_A), layer), vec(D_A),
                  _layer_spec((D_A, D_A), layer), vec(D_A), _layer_spec((D_A, D_A), layer), vec(D_A),
                  vec(D_A), vec(D_A), _layer_spec((D_B, D_B), layer), vec(D_B)],
        out_specs=[row(D_A), row(D_B), state(CONV_WIDTH - 1, D_A), state(1, D_A), state(POOL_BUF, D_B)],
        out_shape=[jax.ShapeDtypeStruct((NP, D_A), F32), jax.ShapeDtypeStruct((NP, D_B), F32),
                   jax.ShapeDtypeStruct((BATCH, CONV_WIDTH - 1, D_A), F32),
                   jax.ShapeDtypeStruct((BATCH, 1, D_A), F32),
                   jax.ShapeDtypeStruct((BATCH, POOL_BUF, D_B), F32)],
        scratch_shapes=[pltpu.VMEM((TT + 8, D_A), F32), pltpu.VMEM((TT + 16, D_B), F32),
                        pltpu.VMEM((8, D_A), F32), pltpu.VMEM((TT, D_A), F32)],
        compiler_params=_params("parallel", "arbitrary"),
        name="mix_prompt",
    )(xag, xb, lw["conv_w"], lw["conv_b"], lw["rg_wa"], lw["rg_ba"], lw["rg_wx"], lw["rg_bx"],
      lw["rg_lambda"], lw["norm_a"], lw["pool_w"], lw["pool_scale"])


def _mix_sample_kernel(xag_ref, xb_ref, sconv_ref, srg_ref, spool_ref,
                       cw_ref, cb_ref, wa_ref, ba_ref, wx_ref, bx_ref, lam_ref, na_ref, pw_ref, ps_ref,
                       ya_ref, yb_ref, convn_ref, rgn_ref, pooln_ref):
    nb = DEC_BATCH
    rows = lambda t: slice(t * nb, (t + 1) * nb)
    xe = [sconv_ref[0, t] for t in range(CONV_WIDTH - 1)] + [xag_ref[rows(t), 0:D_A] for t in range(DEC_SEQ)]
    cw = cw_ref[...]
    xcs = []
    for t in range(DEC_SEQ):
        y = cb_ref[...]
        for jj in range(CONV_WIDTH):
            y = y + cw[jj:jj + 1, :] * xe[t + jj]
        xcs.append(y)
    for t in range(CONV_WIDTH - 1):
        convn_ref[t] = xe[len(xe) - (CONV_WIDTH - 1) + t]
    xc = jnp.concatenate(xcs, axis=0)
    a, u = _rg_gates(xc, wa_ref[...], ba_ref[...], wx_ref[...], bx_ref[...], lam_ref[...])
    h = srg_ref[0]
    hs = []
    for t in range(DEC_SEQ):
        h = a[rows(t)] * h + u[rows(t)]
        hs.append(h)
    rgn_ref[...] = h
    ga = xag_ref[:, D_A:2 * D_A]
    ya_ref[...] = _rms(jax.nn.gelu(ga) * jnp.concatenate(hs, axis=0), na_ref[...])

    pe = [spool_ref[0, t] for t in range(POOL_BUF)] + [xb_ref[rows(t), :] for t in range(DEC_SEQ)]
    for t in range(POOL_BUF):
        pooln_ref[t] = pe[len(pe) - POOL_BUF + t]
    lane = lax.broadcasted_iota(jnp.int32, (nb, D_B), 1)
    pooled = []
    for t in range(DEC_SEQ):
        s = pe[POOL_BUF + t]
        means = []
        for lag in range(1, max(POOL_WINDOWS)):
            s = s + pe[POOL_BUF + t - lag]
            if lag + 1 in POOL_WINDOWS:
                means.append(s / float(min(lag + 1, PAST_LEN + t + 1)))
        pooled.append(_pool_lane_select(lane, means) - pe[POOL_BUF + t])
    pooled = jnp.concatenate(pooled, axis=0).astype(BF16)
    yb_ref[...] = _dot(pooled, pw_ref[...]) * ps_ref[...]


def _mix_sample(xag, xb, sconv_t, srg, spool_t, lw, layer):
    vec = lambda w: _layer_spec((1, w), layer)
    lay = lambda *s: pl.BlockSpec((1,) + s, lambda i: (layer,) + (0,) * len(s))
    full = lambda *s: pl.BlockSpec(s, lambda i: (0,) * len(s))
    return pl.pallas_call(
        _mix_sample_kernel,
        grid=(1,),
        in_specs=[pl.BlockSpec((NS, 2 * D_A), lambda i: (NPT, 0)), pl.BlockSpec((NS, D_B), lambda i: (NPT, 0)),
                  lay(CONV_WIDTH - 1, DEC_BATCH, D_A), lay(DEC_BATCH, D_A), lay(POOL_BUF, DEC_BATCH, D_B),
                  _layer_spec((CONV_WIDTH, D_A), layer), vec(D_A),
                  _layer_spec((D_A, D_A), layer), vec(D_A), _layer_spec((D_A, D_A), layer), vec(D_A),
                  vec(D_A), vec(D_A), _layer_spec((D_B, D_B), layer), vec(D_B)],
        out_specs=[full(NS, D_A), full(NS, D_B), full(CONV_WIDTH - 1, DEC_BATCH, D_A), full(DEC_BATCH, D_A),
                   full(POOL_BUF, DEC_BATCH, D_B)],
        out_shape=[jax.ShapeDtypeStruct((NS, D_A), F32), jax.ShapeDtypeStruct((NS, D_B), F32),
                   jax.ShapeDtypeStruct((CONV_WIDTH - 1, DEC_BATCH, D_A), F32),
                   jax.ShapeDtypeStruct((DEC_BATCH, D_A), F32),
                   jax.ShapeDtypeStruct((POOL_BUF, DEC_BATCH, D_B), F32)],
        compiler_params=_params("arbitrary"),
        name="mix_sample",
    )(xag, xb, sconv_t, srg, spool_t, lw["conv_w"], lw["conv_b"], lw["rg_wa"], lw["rg_ba"], lw["rg_wx"],
      lw["rg_bx"], lw["rg_lambda"], lw["norm_a"], lw["pool_w"], lw["pool_scale"])


def _attn_prompt_kernel(bias_ref, q_ref, kt_ref, v_ref, o_ref, za_s, zb_s, acc_s, run_s):
    assert TQ == 2 * TK
    p = pl.program_id(1)
    i = pl.program_id(2)
    q = q_ref[...]
    lane_q = lax.broadcasted_iota(jnp.int32, (TQ, HEAD_PAIR), 1)
    qs = jnp.concatenate([jnp.where(lane_q < HEAD_DIM, q, jnp.zeros_like(q)),
                          jnp.where(lane_q >= HEAD_DIM, q, jnp.zeros_like(q))], axis=0) * jnp.asarray(SCALE, BF16)
    r = lax.broadcasted_iota(jnp.int32, (2 * TQ, TK), 0) & (TQ - 1)
    c = lax.broadcasted_iota(jnp.int32, (2 * TQ, TK), 1)
    ones = _suffix_ones(TK)
    bias0 = bias_ref[2 * p]
    bias1 = bias_ref[2 * p + 1]

    def scores(j):
        start = pl.multiple_of(j * TK, TK)
        z = _dot(qs, kt_ref[:, pl.ds(start, TK)])
        return jnp.concatenate([z[:TQ] + bias0, z[TQ:] + bias1], axis=0)

    def process(z_ref, j, diag_off):
        z = z_ref[...]
        start = pl.multiple_of(j * TK, TK)
        sp = _softplus(z)
        if diag_off is not None:
            causal = c + diag_off * TK < r
            sp = jnp.where(causal, sp, 0.0)
        run = run_s[...]
        cs = _dot(_hi_lo(sp), ones)
        a = jnp.exp(z - (cs + jnp.concatenate([run] * (TK // LANE), axis=1)))
        if diag_off is not None:
            a = jnp.where(causal, a, 0.0)
        acc_s[...] += _dot(a, v_ref[pl.ds(start, TK), :])
        run_s[...] = run + jnp.broadcast_to(cs[:, 0:1], run.shape)

    acc_s[...] = jnp.zeros_like(acc_s)
    run_s[...] = jnp.zeros_like(run_s)
    t0 = 2 * i + 1
    za_s[...] = scores(t0)
    zb_s[...] = scores(t0 - 1)
    process(za_s, t0, 1)
    za_s[...] = scores(jnp.maximum(t0 - 2, 0))
    process(zb_s, t0 - 1, 0)

    def two_tiles(jj, carry):
        t = t0 - 2 - 2 * jj
        zb_s[...] = scores(t - 1)
        process(za_s, t, None)
        za_s[...] = scores(jnp.maximum(t - 2, 0))
        process(zb_s, t - 1, None)
        return carry

    lax.fori_loop(0, i, two_tiles, 0)
    acc = acc_s[...]
    o_ref[...] = jnp.where(lane_q < HEAD_DIM, acc[:TQ], acc[TQ:])


def _attn_prompt(bias, qb, ktb, vb):
    nq = SEQ // TQ
    return pl.pallas_call(
        _attn_prompt_kernel,
        grid=(BATCH, N_PAIRS, nq),
        in_specs=[pl.BlockSpec(memory_space=pltpu.SMEM),
                  pl.BlockSpec((TQ, HEAD_PAIR), lambda b, p, i: (b * nq + i, p)),
                  pl.BlockSpec((None, HEAD_PAIR, SEQ), lambda b, p, i: (b, p, 0)),
                  pl.BlockSpec((SEQ, HEAD_PAIR), lambda b, p, i: (b, p))],
        out_specs=pl.BlockSpec((TQ, HEAD_PAIR), lambda b, p, i: (b * nq + i, p)),
        out_shape=jax.ShapeDtypeStruct((NP, D_C), F32),
        scratch_shapes=[pltpu.VMEM((2 * TQ, TK), F32), pltpu.VMEM((2 * TQ, TK), F32),
                        pltpu.VMEM((2 * TQ, HEAD_PAIR), F32), pltpu.VMEM((2 * TQ, HEAD_PAIR), F32)],
        compiler_params=_params("parallel", "parallel", "arbitrary"),
        name="attn_prompt",
    )(bias, qb, ktb, vb)


QROWS = 32
N_BLOCKS = N_PAGES + 1


def _attn_decode_kernel(layer, pt_ref, bias_ref, q8_ref, kn_ref, vn_ref, ckt_hbm, cvt_hbm, o_ref,
                        kbuf, vbuf, kn_s, vn_s, sem):
    b = pl.program_id(0)
    slot = b & 1

    def page_copies(seq, slot):
        cps = []
        for pg in range(N_PAGES):
            page = pt_ref[seq * N_PAGES + pg]
            lanes = pl.ds(pg * PAGE_SIZE, PAGE_SIZE)
            cps.append(pltpu.make_async_copy(ckt_hbm.at[layer, page], kbuf.at[slot, :, lanes], sem.at[slot]))
            cps.append(pltpu.make_async_copy(cvt_hbm.at[layer, page], vbuf.at[slot, :, lanes], sem.at[slot]))
        return cps

    @pl.when(b == 0)
    def _():
        kn_s[...] = jnp.zeros_like(kn_s)
        vn_s[...] = jnp.zeros_like(vn_s)
        for cp in page_copies(0, 0):
            cp.start()

    @pl.when(b + 1 < DEC_BATCH)
    def _():
        for cp in page_copies(b + 1, 1 - slot):
            cp.start()

    kn_s[0:DEC_SEQ, :] = kn_ref[0]
    vn_s[0:DEC_SEQ, :] = vn_ref[0]

    rq = lax.broadcasted_iota(jnp.int32, (QROWS, D_C), 0)
    cq = lax.broadcasted_iota(jnp.int32, (QROWS, D_C), 1)
    head_lanes = (cq >> 6) == (rq >> 2)
    q8 = q8_ref[0] * SCALE
    qbd = jnp.where(head_lanes, jnp.concatenate([q8] * (QROWS // 8), axis=0), 0.0)

    rr = lax.broadcasted_iota(jnp.int32, (QROWS, PAGE_SIZE), 0)
    cc = lax.broadcasted_iota(jnp.int32, (QROWS, PAGE_SIZE), 1)
    bias = jnp.zeros((QROWS, PAGE_SIZE), F32)
    for h in range(N_C_HEADS):
        bias = jnp.where((rr >> 2) == h, bias_ref[h], bias)
    new_valid = cc < (rr & 3)
    z_new = _dot_nt(qbd, kn_s[...]) + bias

    for cp in page_copies(b, slot):
        cp.wait()

    z_wide = _dot(qbd, kbuf[slot])
    z_old = jnp.concatenate([z_wide[:, n * PAGE_SIZE:(n + 1) * PAGE_SIZE] + bias for n in range(N_PAGES)], axis=0)
    sp = jnp.concatenate([_softplus(z_old), jnp.where(new_valid, _softplus(z_new), 0.0)], axis=0)
    z = jnp.concatenate([z_old, z_new], axis=0)
    cs = _dot(_hi_lo(sp), _suffix_ones(PAGE_SIZE))

    run = jnp.zeros((QROWS, 1), F32)
    runs = [None] * N_BLOCKS
    for n in range(N_BLOCKS - 1, -1, -1):
        runs[n] = jnp.broadcast_to(run, (QROWS, PAGE_SIZE))
        run = run + cs[n * QROWS:(n + 1) * QROWS, 0:1]
    a = jnp.exp(z - (cs + jnp.concatenate(runs, axis=0)))
    blk = lambda n: a[n * QROWS:(n + 1) * QROWS]

    a_wide = jnp.concatenate([blk(n) for n in range(N_PAGES)], axis=1)
    acc = _dot(jnp.where(new_valid, blk(N_PAGES), 0.0), vn_s[...]) + _dot_nt(a_wide, vbuf[slot])

    acc = jnp.where(head_lanes, acc, 0.0)
    s = acc[0:8] + acc[8:16] + acc[16:24]
    s = s + pltpu.roll(s, 4, 0)
    o_ref[0] = s[0:DEC_SEQ]


def _attn_decode(page_table_flat, bias, q8, kn, vn, cache_kt, cache_vt, layer):
    seq = lambda r: pl.BlockSpec((1, r, D_C), lambda b, pt: (b, 0, 0))
    hbm = pl.BlockSpec(memory_space=pl.ANY)
    grid_spec = pltpu.PrefetchScalarGridSpec(
        num_scalar_prefetch=1,
        grid=(DEC_BATCH,),
        in_specs=[pl.BlockSpec(memory_space=pltpu.SMEM), seq(8), seq(DEC_SEQ), seq(DEC_SEQ), hbm, hbm],
        out_specs=seq(DEC_SEQ),
        scratch_shapes=[pltpu.VMEM((2, D_C, PAST_LEN), F32), pltpu.VMEM((2, D_C, PAST_LEN), F32),
                        pltpu.VMEM((PAGE_SIZE, D_C), F32), pltpu.VMEM((PAGE_SIZE, D_C), F32),
                        pltpu.SemaphoreType.DMA((2,))],
    )
    return pl.pallas_call(
        functools.partial(_attn_decode_kernel, layer),
        grid_spec=grid_spec,
        out_shape=jax.ShapeDtypeStruct((DEC_BATCH, DEC_SEQ, D_C), F32),
        compiler_params=_params("arbitrary"),
        name="attn_decode",
    )(page_table_flat, bias, q8, kn, vn, cache_kt, cache_vt)


MXU_TILE = 256
FF_SPLIT = (D_FF // MXU_TILE + 1) // 2 * MXU_TILE


def _tail_kernel(last, n_h, yap_ref, ybp_ref, op_ref, pep_ref, yas_ref, ybs_ref, os_ref, pes_ref, *refs):
    h_refs = refs[:n_h]
    nc_ref, wout_ref, nf_ref, wg_ref, wu_ref, wd_ref, np_ref, wpg_ref, wpp_ref, nfin_ref = refs[n_h:n_h + 10]
    out_refs = refs[n_h + 10:]
    i = pl.program_id(0)
    is_sample = i == NPT
    pick = lambda s_ref, p_ref: jnp.where(is_sample, s_ref[...], p_ref[...])
    yc = _rms(pick(os_ref, op_ref), nc_ref[...])
    y = jnp.concatenate([pick(yas_ref, yap_ref), pick(ybs_ref, ybp_ref), yc], axis=-1).astype(BF16)
    h = _load_h(h_refs) + _dot(y, wout_ref[...])

    hn = _rms(h, nf_ref[...]).astype(BF16)
    for cols in (slice(0, FF_SPLIT), slice(FF_SPLIT, D_FF)):
        act = (jax.nn.silu(_dot(hn, wg_ref[:, cols])) * _dot(hn, wu_ref[:, cols])).astype(BF16)
        h = h + _dot(act, wd_ref[cols, :])

    hn = _rms(h, np_ref[...]).astype(BF16)
    gate = jax.nn.sigmoid(_dot(hn, wpg_ref[...]))
    h = h + gate * _dot(pick(pes_ref, pep_ref).astype(BF16), wpp_ref[...])

    if not last:
        out_refs[0][...] = h
    else:
        y_out = _rms(h, nfin_ref[...])
        yp_ref, ys_ref = out_refs

        @pl.when(i < NPT)
        def _():
            yp_ref[...] = y_out

        @pl.when(is_sample)
        def _():
            ys_ref[...] = y_out


def _tail(ya_p, yb_p, o_p, pe_p, ya_s, yb_s, o_s, pe_s, hs, w, norm_final, layer):
    last = layer == DEPTH - 1
    n_h = len(hs)
    pi = lambda i: jnp.minimum(i, NPT - 1)
    prow = lambda wd: pl.BlockSpec((TM, wd), lambda i: (pi(i), 0))
    srow = lambda wd: pl.BlockSpec((NS, wd), lambda i: (0, 0), pipeline_mode=pl.Buffered(1))
    row = lambda wd: pl.BlockSpec((TM, wd), lambda i: (i, 0))
    vec = lambda wd: _layer_spec((1, wd), layer)
    if last:
        out_specs = [prow(D_MODEL), pl.BlockSpec((NS, D_MODEL), lambda i: (0, 0))]
        out_shape = [jax.ShapeDtypeStruct((NP, D_MODEL), F32), jax.ShapeDtypeStruct((NS, D_MODEL), F32)]
    else:
        out_specs = [row(D_MODEL)]
        out_shape = [jax.ShapeDtypeStruct((NT, D_MODEL), F32)]
    return pl.pallas_call(
        functools.partial(_tail_kernel, last, n_h),
        grid=(NTT,),
        in_specs=[prow(D_A), prow(D_B), prow(D_C),
                  pl.BlockSpec((None, TM, D_PLE), lambda i: (layer, pi(i), 0)),
                  srow(D_A), srow(D_B), srow(D_C),
                  pl.BlockSpec((None, NS, D_PLE), lambda i: (layer, 0, 0), pipeline_mode=pl.Buffered(1))]
        + _h_specs(n_h == 2)
        + [vec(D_C), _layer_spec((D_MODEL, D_MODEL), layer),
                  vec(D_MODEL), _layer_spec((D_MODEL, D_FF), layer), _layer_spec((D_MODEL, D_FF), layer),
                  _layer_spec((D_FF, D_MODEL), layer),
                  vec(D_MODEL), _layer_spec((D_MODEL, D_MODEL), layer), _layer_spec((D_PLE, D_MODEL), layer),
                  pl.BlockSpec((1, D_MODEL), lambda i: (0, 0), pipeline_mode=pl.Buffered(1))],
        out_specs=out_specs,
        out_shape=out_shape,
        compiler_params=_params("arbitrary"),
        name="tail",
    )(ya_p, yb_p, o_p, pe_p, ya_s, yb_s, o_s, pe_s, *hs,
      w["norm_c"], w["w_out"], w["norm_ffn"], w["w_ff_gate"], w["w_ff_up"], w["w_ff_down"],
      w["norm_ple"], w["w_ple_gate"], w["w_ple_proj"], norm_final)


def _block_diag(w):
    depth, g, n, _ = w.shape
    eye = jnp.eye(g, dtype=w.dtype)
    return jnp.einsum("dgij,gh->dgihj", w, eye).reshape(depth, g * n, g * n)


def kernel(x_prompt, x_sample, cache_k, cache_v, state_conv, state_rglru, state_pool, page_table,
           p_prompt, p_sample, norm_mix, w_in, conv_w, conv_b, rg_wa, rg_ba, rg_wx, rg_bx, rg_lambda,
           norm_a, pool_w, pool_scale, norm_c, sb_bias, w_out, norm_ffn, w_ff_gate, w_ff_up, w_ff_down,
           norm_ple, w_ple_gate, w_ple_proj, norm_final):
    n_pool = cache_k.shape[1]
    tmajor = lambda x: jnp.swapaxes(x, -3, -2)

    hs = (x_prompt.reshape(NP, D_MODEL), tmajor(x_sample).reshape(NS, D_MODEL))
    pe_p = p_prompt.reshape(DEPTH, NP, D_PLE)
    pe_s = tmajor(p_sample).reshape(DEPTH, NS, D_PLE)
    ckt = jnp.transpose(cache_k, (0, 1, 3, 4, 2)).reshape(DEPTH, n_pool, D_C, PAGE_SIZE)
    cvt = jnp.transpose(cache_v, (0, 1, 3, 4, 2)).reshape(DEPTH, n_pool, D_C, PAGE_SIZE)
    pt = page_table.reshape(-1)
    sconv_t = tmajor(state_conv)
    spool_t = tmajor(state_pool)

    vec = lambda x: x.reshape(DEPTH, 1, -1)
    lw = {
        "conv_w": conv_w, "conv_b": vec(conv_b),
        "rg_wa": _block_diag(rg_wa).astype(BF16), "rg_ba": vec(rg_ba),
        "rg_wx": _block_diag(rg_wx).astype(BF16), "rg_bx": vec(rg_bx),
        "rg_lambda": vec(rg_lambda), "norm_a": vec(norm_a),
        "pool_w": _block_diag(pool_w).astype(BF16), "pool_scale": vec(pool_scale),
    }
    w_in_b = w_in.astype(BF16)
    w_kvt_b = jnp.swapaxes(w_in[:, :, S_K:], 1, 2).astype(BF16)
    norm_mix3 = vec(norm_mix)
    tw = {
        "norm_c": vec(norm_c), "w_out": w_out.astype(BF16), "norm_ffn": vec(norm_ffn),
        "w_ff_gate": w_ff_gate.astype(BF16), "w_ff_up": w_ff_up.astype(BF16), "w_ff_down": w_ff_down.astype(BF16),
        "norm_ple": vec(norm_ple), "w_ple_gate": w_ple_gate.astype(BF16), "w_ple_proj": w_ple_proj.astype(BF16),
    }
    norm_final2 = norm_final.reshape(1, D_MODEL)

    kt_all = jnp.zeros((DEPTH, BATCH, D_C, SEQ), F32)
    vt_all = jnp.zeros((DEPTH, BATCH, D_C, SEQ), F32)
    outs = {n: [] for n in ("ks", "vs", "cp", "cs", "rp", "rs", "pp", "ps")}
    for l in range(DEPTH):
        xag, xb, qb, vb, ktb, kt_all, vt_all, qkv_s, kt_s, vt_s = _in_proj(
            hs, norm_mix3, w_in_b, w_kvt_b, kt_all, vt_all, l)
        ya_p, yb_p, conv_p, rg_p, pool_p = _mix_prompt(xag, xb, lw, l)
        ya_s, yb_s, conv_s, rg_s, pool_s = _mix_sample(xag, xb, sconv_t, state_rglru, spool_t, lw, l)
        o_p = _attn_prompt(sb_bias[l], qb, ktb, vb)

        bmajor = lambda x: tmajor(x.reshape(DEC_SEQ, DEC_BATCH, D_C))
        q_s = bmajor(qkv_s[:, :D_C])
        o_s = _attn_decode(pt, sb_bias[l], jnp.concatenate([q_s, q_s], axis=1),
                           bmajor(qkv_s[:, D_C:2 * D_C]), bmajor(qkv_s[:, 2 * D_C:]), ckt, cvt, l)
        o_s = tmajor(o_s).reshape(NS, D_C)

        res = _tail(ya_p, yb_p, o_p, pe_p, ya_s, yb_s, o_s, pe_s, hs, tw, norm_final2, l)
        hs = (res[0],)

        outs["ks"].append(kt_s)
        outs["vs"].append(vt_s)
        outs["cp"].append(conv_p)
        outs["cs"].append(conv_s)
        outs["rp"].append(rg_p.reshape(BATCH, D_A))
        outs["rs"].append(rg_s)
        outs["pp"].append(pool_p)
        outs["ps"].append(pool_s)

    y_prompt = res[0].reshape(BATCH, SEQ, D_MODEL)
    y_sample = tmajor(res[1].reshape(DEC_SEQ, DEC_BATCH, D_MODEL))
    st = lambda n: jnp.stack(outs[n])
    kv_prompt = lambda x: jnp.transpose(x.reshape(DEPTH, BATCH, N_C_HEADS, HEAD_DIM, SEQ), (0, 1, 4, 2, 3))
    kv_sample = lambda x: jnp.transpose(x.reshape(DEPTH, DEC_SEQ, N_C_HEADS, HEAD_DIM, DEC_BATCH), (0, 4, 1, 2, 3))
    return (y_prompt, y_sample, kv_prompt(kt_all), kv_prompt(vt_all), kv_sample(st("ks")), kv_sample(st("vs")),
            st("cp"), tmajor(st("cs")), st("rp"), st("rs"), st("pp"), tmajor(st("ps")))
```

```python
import functools

import jax
import jax.numpy as jnp
from jax import lax
from jax.experimental import pallas as pl
from jax.experimental.pallas import tpu as pltpu

F32 = jnp.float32
BF16 = jnp.bfloat16

D_MODEL = 1024
BATCH = 4
SEQ = 4096
DEPTH = 4
DEC_BATCH = 128
DEC_SEQ = 4
PAST_LEN = 2048
PAGE_SIZE = 128
N_PAGES = PAST_LEN // PAGE_SIZE
HEAD_DIM = 64
D_A = 384
D_B = 256
D_C = 384
N_C_HEADS = 6
POOL_WINDOWS = (2, 4, 8, 16)
POOL_GROUP = 64
POOL_BUF = 15
CONV_WIDTH = 4
RG_C = 8.0
D_IN = 2 * D_A + D_B + 3 * D_C
D_FF = 2816
D_PLE = 256
EPS = 1e-6
SCALE = HEAD_DIM ** -0.5
LOG2E = 1.4426950408889634

NP = BATCH * SEQ
NS = DEC_BATCH * DEC_SEQ
NT = NP + NS
TM = 512
NPT = NP // TM
NTT = NT // TM
TT = 512
TQ = 512
TK = 256
LANE = 128
HEAD_PAIR = 2 * HEAD_DIM
N_PAIRS = D_C // HEAD_PAIR
KT_ROWS = 2 * HEAD_PAIR
BIAS_ROWS = 3
VMEM_LIMIT = 56 * 1024 * 1024


def _rms(x, g):
    ms = jnp.mean(x * x, axis=-1, keepdims=True)
    return x * lax.rsqrt(ms + EPS) * g


def _dot(a, b):
    return lax.dot_general(a, b, (((1,), (0,)), ((), ())), preferred_element_type=F32)


def _dot_nt(a, b):
    return lax.dot_general(a, b, (((1,), (1,)), ((), ())), preferred_element_type=F32)


def _softplus(z):
    return jnp.maximum(z, 0.0) + jnp.log(1.0 + jnp.exp2(jnp.abs(z) * (-LOG2E)))


def _hi_lo(x):
    hi = x.astype(BF16).astype(F32)
    return jnp.concatenate([hi, x - hi], axis=1)


def _suffix_ones(n):
    r = lax.broadcasted_iota(jnp.int32, (2 * n, n), 0) & (n - 1)
    c = lax.broadcasted_iota(jnp.int32, (2 * n, n), 1)
    return jnp.where(r >= c, 1.0, 0.0).astype(BF16)


def _layer_spec(shape, layer):
    nd = len(shape)
    return pl.BlockSpec((None,) + tuple(shape), lambda *_: (layer,) + (0,) * nd,
                        pipeline_mode=pl.Buffered(1))


def _params(*sem):
    return pltpu.CompilerParams(dimension_semantics=sem, vmem_limit_bytes=VMEM_LIMIT)


S_XB = 2 * D_A
S_Q = S_XB + D_B
S_K = S_Q + D_C
S_V = S_K + D_C


def _load_h(h_refs):
    if len(h_refs) == 1:
        return h_refs[0][...]
    return jnp.where(pl.program_id(0) == NPT, h_refs[1][...], h_refs[0][...])


def _h_specs(split):
    if not split:
        return [pl.BlockSpec((TM, D_MODEL), lambda i: (i, 0))]
    return [pl.BlockSpec((TM, D_MODEL), lambda i: (jnp.minimum(i, NPT - 1), 0)),
            pl.BlockSpec((NS, D_MODEL), lambda i: (0, 0), pipeline_mode=pl.Buffered(1))]


def _in_proj_kernel(n_h, *refs):
    h_refs = refs[:n_h]
    (g_ref, w_ref, wkvt_ref, kt_prev, vt_prev,
     xag_ref, xb_ref, qb_ref, vb_ref, ktb_ref, kt_ref, vt_ref, qkvs_ref, kts_ref, vts_ref) = refs[n_h:]
    del kt_prev, vt_prev
    i = pl.program_id(0)
    hn = _rms(_load_h(h_refs), g_ref[...]).astype(BF16)
    u = _dot(hn, w_ref[:, :S_K])
    v = _dot(hn, w_ref[:, S_V:])
    kvt = _dot_nt(wkvt_ref[...], hn)
    xag_ref[...] = u[:, :S_XB]
    xb_ref[...] = u[:, S_XB:S_Q]

    @pl.when(i < NPT)
    def _():
        qb_ref[...] = u[:, S_Q:].astype(BF16)
        vb_ref[...] = v.astype(BF16)
        kt = kvt[:D_C]
        kt_ref[...] = kt
        vt_ref[...] = kvt[D_C:]
        row = lax.broadcasted_iota(jnp.int32, (HEAD_PAIR, TM), 0)
        ones_rows = jnp.where(row < BIAS_ROWS, 1.0, 0.0).astype(BF16)
        for p in range(N_PAIRS):
            ktb_ref[p * KT_ROWS:p * KT_ROWS + HEAD_PAIR, :] = kt[p * HEAD_PAIR:(p + 1) * HEAD_PAIR].astype(BF16)
            ktb_ref[p * KT_ROWS + HEAD_PAIR:(p + 1) * KT_ROWS, :] = ones_rows

    @pl.when(i == NPT)
    def _():
        qkvs_ref[:, :D_C] = u[:, S_Q:]
        qkvs_ref[:, D_C:2 * D_C] = _dot(hn, w_ref[:, S_K:S_V])
        qkvs_ref[:, 2 * D_C:] = v
        for t in range(DEC_SEQ):
            kts_ref[t] = kvt[:D_C, t * DEC_BATCH:(t + 1) * DEC_BATCH]
            vts_ref[t] = kvt[D_C:, t * DEC_BATCH:(t + 1) * DEC_BATCH]


def _in_proj(hs, norm_mix, w_in_b, w_kvt_b, kt_all, vt_all, layer):
    n_h = len(hs)
    tiles_per_seq = SEQ // TM
    pi = lambda i: jnp.minimum(i, NPT - 1)
    row = lambda w: pl.BlockSpec((TM, w), lambda i: (i, 0))
    prow = lambda w: pl.BlockSpec((TM, w), lambda i: (pi(i), 0))
    full = lambda *s: pl.BlockSpec(s, lambda i: (0,) * len(s))
    kv_t = pl.BlockSpec((None, None, D_C, TM), lambda i: (layer, pi(i) // tiles_per_seq, 0, pi(i) % tiles_per_seq))
    ktb = pl.BlockSpec((None, N_PAIRS * KT_ROWS, TM), lambda i: (pi(i) // tiles_per_seq, 0, pi(i) % tiles_per_seq))
    any_spec = pl.BlockSpec(memory_space=pl.ANY)
    kv_shape = jax.ShapeDtypeStruct((DEPTH, BATCH, D_C, SEQ), F32)
    return pl.pallas_call(
        functools.partial(_in_proj_kernel, n_h),
        grid=(NTT,),
        in_specs=_h_specs(n_h == 2) + [_layer_spec((1, D_MODEL), layer), _layer_spec((D_MODEL, D_IN), layer),
                                       _layer_spec((2 * D_C, D_MODEL), layer), any_spec, any_spec],
        out_specs=[row(2 * D_A), row(D_B), prow(D_C), prow(D_C), ktb, kv_t, kv_t,
                   full(NS, 3 * D_C), full(DEC_SEQ, D_C, DEC_BATCH), full(DEC_SEQ, D_C, DEC_BATCH)],
        out_shape=[jax.ShapeDtypeStruct((NT, 2 * D_A), F32), jax.ShapeDtypeStruct((NT, D_B), F32),
                   jax.ShapeDtypeStruct((NP, D_C), BF16), jax.ShapeDtypeStruct((NP, D_C), BF16),
                   jax.ShapeDtypeStruct((BATCH, N_PAIRS * KT_ROWS, SEQ), BF16), kv_shape, kv_shape,
                   jax.ShapeDtypeStruct((NS, 3 * D_C), F32),
                   jax.ShapeDtypeStruct((DEC_SEQ, D_C, DEC_BATCH), F32),
                   jax.ShapeDtypeStruct((DEC_SEQ, D_C, DEC_BATCH), F32)],
        input_output_aliases={n_h + 3: 5, n_h + 4: 6},
        compiler_params=_params("arbitrary"),
        name="in_proj",
    )(*hs, norm_mix, w_in_b, w_kvt_b, kt_all, vt_all)


def _rg_gates(xc, wa, ba, wx, bx, lam):
    xcb = xc.astype(BF16)
    r = jax.nn.sigmoid(_dot(xcb, wa) + ba)
    ig = jax.nn.sigmoid(_dot(xcb, wx) + bx)
    log_a = RG_C * r * jax.nn.log_sigmoid(lam)
    a = jnp.exp(log_a)
    om = 1.0 - a * a
    u = jnp.where(om > 0.0, om * lax.rsqrt(om), 0.0) * (ig * xc)
    return a, u


def _pool_lane_select(lane, per_window):
    out = per_window[-1]
    for g in range(len(POOL_WINDOWS) - 2, -1, -1):
        out = jnp.where(lane < (g + 1) * POOL_GROUP, per_window[g], out)
    return out


def _mix_prompt_kernel(xag_ref, xb_ref, cw_ref, cb_ref, wa_ref, ba_ref, wx_ref, bx_ref, lam_ref,
                       na_ref, pw_ref, ps_ref,
                       ya_ref, yb_ref, convn_ref, rgn_ref, pooln_ref,
                       xbuf, pbuf, hcar, hs_s):
    j = pl.program_id(1)
    hist_a = 8
    hist_b = 16

    @pl.when(j == 0)
    def _():
        xbuf[0:hist_a, :] = jnp.zeros((hist_a, D_A), F32)
        pbuf[0:hist_b, :] = jnp.zeros((hist_b, D_B), F32)
        hcar[...] = jnp.zeros_like(hcar)

    xa = xag_ref[:, 0:D_A]
    ga = xag_ref[:, D_A:2 * D_A]
    xbuf[hist_a:hist_a + TT, :] = xa
    cw = cw_ref[...]
    assert CONV_WIDTH == 4
    e = xbuf[...]
    e1 = pltpu.roll(e, 1, 0)
    p2 = pltpu.roll(cw[1:2, :] * e + cw[0:1, :] * e1, 2, 0)
    xc = (cb_ref[...] + cw[3:4, :] * e + cw[2:3, :] * e1 + p2)[hist_a:hist_a + TT]
    convn_ref[0] = xbuf[hist_a + TT - (CONV_WIDTH - 1):hist_a + TT, :]
    xbuf[0:hist_a, :] = xbuf[TT:TT + hist_a, :]

    a, u = _rg_gates(xc, wa_ref[...], ba_ref[...], wx_ref[...], bx_ref[...], lam_ref[...])
    sub_rows = 8
    a3 = a.reshape(TT // sub_rows, sub_rows, D_A)
    u3 = u.reshape(TT // sub_rows, sub_rows, D_A)
    sub = lax.broadcasted_iota(jnp.int32, a3.shape, 1)
    d = 1
    while d < sub_rows:
        m = sub >= d
        a_s = jnp.where(m, pltpu.roll(a3, d, 1), 1.0)
        u_s = jnp.where(m, pltpu.roll(u3, d, 1), 0.0)
        u3 = a3 * u_s + u3
        a3 = a3 * a_s
        d *= 2
    h = hcar[0:1, :]
    for g in range(TT // sub_rows):
        hs_g = u3[g] + a3[g] * h
        hs_s[g * sub_rows:(g + 1) * sub_rows, :] = hs_g
        h = hs_g[sub_rows - 1:sub_rows, :]
    hcar[0:1, :] = h
    rgn_ref[0] = h
    hs = hs_s[...]
    ya_ref[...] = _rms(jax.nn.gelu(ga) * hs, na_ref[...])

    xb = xb_ref[...]
    pbuf[hist_b:hist_b + TT, :] = xb
    lane = lax.broadcasted_iota(jnp.int32, (TT, D_B), 1)
    pos = j * TT + lax.broadcasted_iota(jnp.int32, (TT, D_B), 0)
    assert POOL_WINDOWS == (2, 4, 8, 16) and D_B == 2 * LANE and 2 * POOL_GROUP == LANE
    e = pbuf[...]
    s2 = e + pltpu.roll(e, 1, 0)
    s4 = s2 + pltpu.roll(s2, 2, 0)
    s4_hi = s4[:, LANE:]
    s8_hi = s4_hi + pltpu.roll(s4_hi, 4, 0)
    s16_hi = s8_hi + pltpu.roll(s8_hi, 8, 0)
    tile = lambda x: x[hist_b:hist_b + TT]
    lane_h = lax.broadcasted_iota(jnp.int32, (TT, LANE), 1)
    wsum = jnp.concatenate(
        [jnp.where(lane_h < POOL_GROUP, tile(s2[:, :LANE]), tile(s4[:, :LANE])),
         jnp.where(lane_h < POOL_GROUP, tile(s8_hi), tile(s16_hi))], axis=1)
    win = _pool_lane_select(lane, [jnp.full((TT, D_B), w, jnp.int32) for w in POOL_WINDOWS])
    cnt = jnp.minimum(win, pos + 1).astype(F32)
    pooled = (wsum / cnt - xb).astype(BF16)
    yb_ref[...] = _dot(pooled, pw_ref[...]) * ps_ref[...]
    pooln_ref[0] = pbuf[hist_b + TT - POOL_BUF:hist_b + TT, :]
    pbuf[0:hist_b, :] = pbuf[TT:TT + hist_b, :]


def _mix_prompt(xag, xb, lw, layer):
    nj = SEQ // TT
    row = lambda w: pl.BlockSpec((TT, w), lambda b, j: (b * nj + j, 0))
    vec = lambda w: _layer_spec((1, w), layer)
    state = lambda r, w: pl.BlockSpec((1, r, w), lambda b, j: (b, 0, 0))
    return pl.pallas_call(
        _mix_prompt_kernel,
        grid=(BATCH, nj),
        in_specs=[row(2 * D_A), row(D_B), _layer_spec((CONV_WIDTH, D_A), layer), vec(D_A),
                  _layer_spec((D_A, D_A), layer), vec(D_A), _layer_spec((D_A, D_A), layer), vec(D_A),
                  vec(D_A), vec(D_A), _layer_spec((D_B, D_B), layer), vec(D_B)],
        out_specs=[row(D_A), row(D_B), state(CONV_WIDTH - 1, D_A), state(1, D_A), state(POOL_BUF, D_B)],
        out_shape=[jax.ShapeDtypeStruct((NP, D_A), F32), jax.ShapeDtypeStruct((NP, D_B), F32),
                   jax.ShapeDtypeStruct((BATCH, CONV_WIDTH - 1, D_A), F32),
                   jax.ShapeDtypeStruct((BATCH, 1, D_A), F32),
                   jax.ShapeDtypeStruct((BATCH, POOL_BUF, D_B), F32)],
        scratch_shapes=[pltpu.VMEM((TT + 8, D_A), F32), pltpu.VMEM((TT + 16, D_B), F32),
                        pltpu.VMEM((8, D_A), F32), pltpu.VMEM((TT, D_A), F32)],
        compiler_params=_params("parallel", "arbitrary"),
        name="mix_prompt",
    )(xag, xb, lw["conv_w"], lw["conv_b"], lw["rg_wa"], lw["rg_ba"], lw["rg_wx"], lw["rg_bx"],
      lw["rg_lambda"], lw["norm_a"], lw["pool_w"], lw["pool_scale"])


def _mix_sample_kernel(xag_ref, xb_ref, sconv_ref, srg_ref, spool_ref,
                       cw_ref, cb_ref, wa_ref, ba_ref, wx_ref, bx_ref, lam_ref, na_ref, pw_ref, ps_ref,
                       ya_ref, yb_ref, convn_ref, rgn_ref, pooln_ref):
    nb = DEC_BATCH
    rows = lambda t: slice(t * nb, (t + 1) * nb)
    xe = [sconv_ref[0, t] for t in range(CONV_WIDTH - 1)] + [xag_ref[rows(t), 0:D_A] for t in range(DEC_SEQ)]
    cw = cw_ref[...]
    xcs = []
    for t in range(DEC_SEQ):
        y = cb_ref[...]
        for jj in range(CONV_WIDTH):
            y = y + cw[jj:jj + 1, :] * xe[t + jj]
        xcs.append(y)
    for t in range(CONV_WIDTH - 1):
        convn_ref[t] = xe[len(xe) - (CONV_WIDTH - 1) + t]
    xc = jnp.concatenate(xcs, axis=0)
    a, u = _rg_gates(xc, wa_ref[...], ba_ref[...], wx_ref[...], bx_ref[...], lam_ref[...])
    h = srg_ref[0]
    hs = []
    for t in range(DEC_SEQ):
        h = a[rows(t)] * h + u[rows(t)]
        hs.append(h)
    rgn_ref[...] = h
    ga = xag_ref[:, D_A:2 * D_A]
    ya_ref[...] = _rms(jax.nn.gelu(ga) * jnp.concatenate(hs, axis=0), na_ref[...])

    pe = [spool_ref[0, t] for t in range(POOL_BUF)] + [xb_ref[rows(t), :] for t in range(DEC_SEQ)]
    for t in range(POOL_BUF):
        pooln_ref[t] = pe[len(pe) - POOL_BUF + t]
    lane = lax.broadcasted_iota(jnp.int32, (nb, D_B), 1)
    pooled = []
    for t in range(DEC_SEQ):
        s = pe[POOL_BUF + t]
        means = []
        for lag in range(1, max(POOL_WINDOWS)):
            s = s + pe[POOL_BUF + t - lag]
            if lag + 1 in POOL_WINDOWS:
                means.append(s / float(min(lag + 1, PAST_LEN + t + 1)))
        pooled.append(_pool_lane_select(lane, means) - pe[POOL_BUF + t])
    pooled = jnp.concatenate(pooled, axis=0).astype(BF16)
    yb_ref[...] = _dot(pooled, pw_ref[...]) * ps_ref[...]


def _mix_sample(xag, xb, sconv_t, srg, spool_t, lw, layer):
    vec = lambda w: _layer_spec((1, w), layer)
    lay = lambda *s: pl.BlockSpec((1,) + s, lambda i: (layer,) + (0,) * len(s))
    full = lambda *s: pl.BlockSpec(s, lambda i: (0,) * len(s))
    return pl.pallas_call(
        _mix_sample_kernel,
        grid=(1,),
        in_specs=[pl.BlockSpec((NS, 2 * D_A), lambda i: (NPT, 0)), pl.BlockSpec((NS, D_B), lambda i: (NPT, 0)),
                  lay(CONV_WIDTH - 1, DEC_BATCH, D_A), lay(DEC_BATCH, D_A), lay(POOL_BUF, DEC_BATCH, D_B),
                  _layer_spec((CONV_WIDTH, D_A), layer), vec(D_A),
                  _layer_spec((D_A, D_A), layer), vec(D_A), _layer_spec((D_A, D_A), layer), vec(D_A),
                  vec(D_A), vec(D_A), _layer_spec((D_B, D_B), layer), vec(D_B)],
        out_specs=[full(NS, D_A), full(NS, D_B), full(CONV_WIDTH - 1, DEC_BATCH, D_A), full(DEC_BATCH, D_A),
                   full(POOL_BUF, DEC_BATCH, D_B)],
        out_shape=[jax.ShapeDtypeStruct((NS, D_A), F32), jax.ShapeDtypeStruct((NS, D_B), F32),
                   jax.ShapeDtypeStruct((CONV_WIDTH - 1, DEC_BATCH, D_A), F32),
                   jax.ShapeDtypeStruct((DEC_BATCH, D_A), F32),
                   jax.ShapeDtypeStruct((POOL_BUF, DEC_BATCH, D_B), F32)],
        compiler_params=_params("arbitrary"),
        name="mix_sample",
    )(xag, xb, sconv_t, srg, spool_t, lw["conv_w"], lw["conv_b"], lw["rg_wa"], lw["rg_ba"], lw["rg_wx"],
      lw["rg_bx"], lw["rg_lambda"], lw["norm_a"], lw["pool_w"], lw["pool_scale"])


def _attn_prompt_kernel(bias_ref, q_ref, kt_ref, v_ref, o_ref, za_s, zb_s, acc_s, run_s):
    assert TQ == 2 * TK
    p = pl.program_id(1)
    i = pl.program_id(2)
    q = q_ref[...]
    lane_q = lax.broadcasted_iota(jnp.int32, (TQ, HEAD_PAIR), 1)
    qs = jnp.concatenate([jnp.where(lane_q < HEAD_DIM, q, jnp.zeros_like(q)),
                          jnp.where(lane_q >= HEAD_DIM, q, jnp.zeros_like(q))], axis=0) * jnp.asarray(SCALE, BF16)

    def offset_lanes(bias):
        b = jnp.full((TQ, HEAD_PAIR), bias, F32)
        terms = []
        for _ in range(BIAS_ROWS):
            terms.append(b.astype(BF16).astype(F32))
            b = b - terms[-1]
        out = jnp.zeros((TQ, HEAD_PAIR), F32)
        for n in range(BIAS_ROWS - 1, -1, -1):
            out = jnp.where(lane_q == n, terms[n], out)
        return out.astype(BF16)

    qs = jnp.concatenate([qs, jnp.concatenate([offset_lanes(bias_ref[2 * p]), offset_lanes(bias_ref[2 * p + 1])],
                                              axis=0)], axis=1)
    r = lax.broadcasted_iota(jnp.int32, (2 * TQ, TK), 0) & (TQ - 1)
    c = lax.broadcasted_iota(jnp.int32, (2 * TQ, TK), 1)
    ones = _suffix_ones(TK)

    def scores(j):
        start = pl.multiple_of(j * TK, TK)
        return _dot(qs, kt_ref[:, pl.ds(start, TK)])

    def process(z_ref, j, diag_off):
        z = z_ref[...]
        start = pl.multiple_of(j * TK, TK)
        sp = _softplus(z)
        if diag_off is not None:
            causal = c + diag_off * TK < r
            sp = jnp.where(causal, sp, 0.0)
        run = run_s[...]
        cs = _dot(_hi_lo(sp), ones)
        a = jnp.exp(z - (cs + jnp.concatenate([run] * (TK // LANE), axis=1)))
        if diag_off is not None:
            a = jnp.where(causal, a, 0.0)
        acc_s[...] += _dot(a, v_ref[pl.ds(start, TK), :])
        run_s[...] = run + jnp.broadcast_to(cs[:, 0:1], run.shape)

    def process_last_tile(j):
        lower = lambda x: jnp.concatenate([x[TK:TQ], x[TQ + TK:]], axis=0)
        start = pl.multiple_of(j * TK, TK)
        z = _dot(lower(qs), kt_ref[:, pl.ds(start, TK)])
        rows = lax.broadcasted_iota(jnp.int32, (TQ, TK), 0) & (TK - 1)
        causal = lax.broadcasted_iota(jnp.int32, (TQ, TK), 1) < rows
        sp = jnp.where(causal, _softplus(z), 0.0)
        cs = _dot(_hi_lo(sp), ones)
        a = jnp.where(causal, jnp.exp(z - cs), 0.0)
        av = _dot(a, v_ref[pl.ds(start, TK), :])
        tot = jnp.broadcast_to(cs[:, 0:1], (TQ, HEAD_PAIR))
        zeros = jnp.zeros((TK, HEAD_PAIR), F32)
        acc_s[...] = jnp.concatenate([zeros, av[:TK], zeros, av[TK:]], axis=0)
        run_s[...] = jnp.concatenate([zeros, tot[:TK], zeros, tot[TK:]], axis=0)

    t0 = 2 * i + 1
    zb_s[...] = scores(t0 - 1)
    process_last_tile(t0)
    za_s[...] = scores(jnp.maximum(t0 - 2, 0))
    process(zb_s, t0 - 1, 0)

    def two_tiles(jj, carry):
        t = t0 - 2 - 2 * jj
        zb_s[...] = scores(t - 1)
        process(za_s, t, None)
        za_s[...] = scores(jnp.maximum(t - 2, 0))
        process(zb_s, t - 1, None)
        return carry

    lax.fori_loop(0, i, two_tiles, 0)
    acc = acc_s[...]
    o_ref[...] = jnp.where(lane_q < HEAD_DIM, acc[:TQ], acc[TQ:])


def _attn_prompt(bias, qb, ktb, vb):
    nq = SEQ // TQ
    return pl.pallas_call(
        _attn_prompt_kernel,
        grid=(BATCH, N_PAIRS, nq),
        in_specs=[pl.BlockSpec(memory_space=pltpu.SMEM),
                  pl.BlockSpec((TQ, HEAD_PAIR), lambda b, p, i: (b * nq + i, p)),
                  pl.BlockSpec((None, KT_ROWS, SEQ), lambda b, p, i: (b, p, 0)),
                  pl.BlockSpec((SEQ, HEAD_PAIR), lambda b, p, i: (b, p))],
        out_specs=pl.BlockSpec((TQ, HEAD_PAIR), lambda b, p, i: (b * nq + i, p)),
        out_shape=jax.ShapeDtypeStruct((NP, D_C), F32),
        scratch_shapes=[pltpu.VMEM((2 * TQ, TK), F32), pltpu.VMEM((2 * TQ, TK), F32),
                        pltpu.VMEM((2 * TQ, HEAD_PAIR), F32), pltpu.VMEM((2 * TQ, HEAD_PAIR), F32)],
        compiler_params=_params("parallel", "parallel", "arbitrary"),
        name="attn_prompt",
    )(bias, qb, ktb, vb)


QROWS = 32
N_BLOCKS = N_PAGES + 1


def _attn_decode_kernel(layer, pt_ref, bias_ref, q8_ref, kn_ref, vn_ref, ckt_hbm, cvt_hbm, o_ref,
                        kbuf, vbuf, kn_s, vn_s, sem):
    b = pl.program_id(0)
    slot = b & 1

    def page_copies(seq, slot):
        cps = []
        for pg in range(N_PAGES):
            page = pt_ref[seq * N_PAGES + pg]
            lanes = pl.ds(pg * PAGE_SIZE, PAGE_SIZE)
            cps.append(pltpu.make_async_copy(ckt_hbm.at[layer, page], kbuf.at[slot, :, lanes], sem.at[slot]))
            cps.append(pltpu.make_async_copy(cvt_hbm.at[layer, page], vbuf.at[slot, :, lanes], sem.at[slot]))
        return cps

    @pl.when(b == 0)
    def _():
        kn_s[...] = jnp.zeros_like(kn_s)
        vn_s[...] = jnp.zeros_like(vn_s)
        for cp in page_copies(0, 0):
            cp.start()

    @pl.when(b + 1 < DEC_BATCH)
    def _():
        for cp in page_copies(b + 1, 1 - slot):
            cp.start()

    kn_s[0:DEC_SEQ, :] = kn_ref[0]
    vn_s[0:DEC_SEQ, :] = vn_ref[0]

    rq = lax.broadcasted_iota(jnp.int32, (QROWS, D_C), 0)
    cq = lax.broadcasted_iota(jnp.int32, (QROWS, D_C), 1)
    head_lanes = (cq >> 6) == (rq >> 2)
    q8 = q8_ref[0] * SCALE
    qbd = jnp.where(head_lanes, jnp.concatenate([q8] * (QROWS // 8), axis=0), 0.0)

    rr = lax.broadcasted_iota(jnp.int32, (QROWS, PAGE_SIZE), 0)
    cc = lax.broadcasted_iota(jnp.int32, (QROWS, PAGE_SIZE), 1)
    bias = jnp.zeros((QROWS, PAGE_SIZE), F32)
    for h in range(N_C_HEADS):
        bias = jnp.where((rr >> 2) == h, bias_ref[h], bias)
    new_valid = cc < (rr & 3)
    z_new = _dot_nt(qbd, kn_s[...]) + bias

    for cp in page_copies(b, slot):
        cp.wait()

    z_wide = _dot(qbd, kbuf[slot])
    z_old = jnp.concatenate([z_wide[:, n * PAGE_SIZE:(n + 1) * PAGE_SIZE] + bias for n in range(N_PAGES)], axis=0)
    sp = jnp.concatenate([_softplus(z_old), jnp.where(new_valid, _softplus(z_new), 0.0)], axis=0)
    z = jnp.concatenate([z_old, z_new], axis=0)
    cs = _dot(_hi_lo(sp), _suffix_ones(PAGE_SIZE))

    run = jnp.zeros((QROWS, 1), F32)
    runs = [None] * N_BLOCKS
    for n in range(N_BLOCKS - 1, -1, -1):
        runs[n] = jnp.broadcast_to(run, (QROWS, PAGE_SIZE))
        run = run + cs[n * QROWS:(n + 1) * QROWS, 0:1]
    a = jnp.exp(z - (cs + jnp.concatenate(runs, axis=0)))
    blk = lambda n: a[n * QROWS:(n + 1) * QROWS]

    a_wide = jnp.concatenate([blk(n) for n in range(N_PAGES)], axis=1)
    acc = _dot(jnp.where(new_valid, blk(N_PAGES), 0.0), vn_s[...]) + _dot_nt(a_wide, vbuf[slot])

    acc = jnp.where(head_lanes, acc, 0.0)
    s = acc[0:8] + acc[8:16] + acc[16:24]
    s = s + pltpu.roll(s, 4, 0)
    o_ref[0] = s[0:DEC_SEQ]


def _attn_decode(page_table_flat, bias, q8, kn, vn, cache_kt, cache_vt, layer):
    seq = lambda r: pl.BlockSpec((1, r, D_C), lambda b, pt: (b, 0, 0))
    hbm = pl.BlockSpec(memory_space=pl.ANY)
    grid_spec = pltpu.PrefetchScalarGridSpec(
        num_scalar_prefetch=1,
        grid=(DEC_BATCH,),
        in_specs=[pl.BlockSpec(memory_space=pltpu.SMEM), seq(8), seq(DEC_SEQ), seq(DEC_SEQ), hbm, hbm],
        out_specs=seq(DEC_SEQ),
        scratch_shapes=[pltpu.VMEM((2, D_C, PAST_LEN), F32), pltpu.VMEM((2, D_C, PAST_LEN), F32),
                        pltpu.VMEM((PAGE_SIZE, D_C), F32), pltpu.VMEM((PAGE_SIZE, D_C), F32),
                        pltpu.SemaphoreType.DMA((2,))],
    )
    return pl.pallas_call(
        functools.partial(_attn_decode_kernel, layer),
        grid_spec=grid_spec,
        out_shape=jax.ShapeDtypeStruct((DEC_BATCH, DEC_SEQ, D_C), F32),
        compiler_params=_params("arbitrary"),
        name="attn_decode",
    )(page_table_flat, bias, q8, kn, vn, cache_kt, cache_vt)


MXU_TILE = 256
FF_SPLIT = (D_FF // MXU_TILE + 1) // 2 * MXU_TILE


def _tail_kernel(last, n_h, yap_ref, ybp_ref, op_ref, pep_ref, yas_ref, ybs_ref, os_ref, pes_ref, *refs):
    h_refs = refs[:n_h]
    nc_ref, wout_ref, nf_ref, wg_ref, wu_ref, wd_ref, np_ref, wpg_ref, wpp_ref, nfin_ref = refs[n_h:n_h + 10]
    out_refs = refs[n_h + 10:]
    i = pl.program_id(0)
    is_sample = i == NPT
    pick = lambda s_ref, p_ref: jnp.where(is_sample, s_ref[...], p_ref[...])
    yc = _rms(pick(os_ref, op_ref), nc_ref[...])
    y = jnp.concatenate([pick(yas_ref, yap_ref), pick(ybs_ref, ybp_ref), yc], axis=-1).astype(BF16)
    h = _load_h(h_refs) + _dot(y, wout_ref[...])

    hn = _rms(h, nf_ref[...]).astype(BF16)
    for cols in (slice(0, FF_SPLIT), slice(FF_SPLIT, D_FF)):
        act = (jax.nn.silu(_dot(hn, wg_ref[:, cols])) * _dot(hn, wu_ref[:, cols])).astype(BF16)
        h = h + _dot(act, wd_ref[cols, :])

    hn = _rms(h, np_ref[...]).astype(BF16)
    gate = jax.nn.sigmoid(_dot(hn, wpg_ref[...]))
    h = h + gate * _dot(pick(pes_ref, pep_ref).astype(BF16), wpp_ref[...])

    if not last:
        out_refs[0][...] = h
    else:
        y_out = _rms(h, nfin_ref[...])
        yp_ref, ys_ref = out_refs

        @pl.when(i < NPT)
        def _():
            yp_ref[...] = y_out

        @pl.when(is_sample)
        def _():
            ys_ref[...] = y_out


def _tail(ya_p, yb_p, o_p, pe_p, ya_s, yb_s, o_s, pe_s, hs, w, norm_final, layer):
    last = layer == DEPTH - 1
    n_h = len(hs)
    pi = lambda i: jnp.minimum(i, NPT - 1)
    prow = lambda wd: pl.BlockSpec((TM, wd), lambda i: (pi(i), 0))
    srow = lambda wd: pl.BlockSpec((NS, wd), lambda i: (0, 0), pipeline_mode=pl.Buffered(1))
    row = lambda wd: pl.BlockSpec((TM, wd), lambda i: (i, 0))
    vec = lambda wd: _layer_spec((1, wd), layer)
    if last:
        out_specs = [prow(D_MODEL), pl.BlockSpec((NS, D_MODEL), lambda i: (0, 0))]
        out_shape = [jax.ShapeDtypeStruct((NP, D_MODEL), F32), jax.ShapeDtypeStruct((NS, D_MODEL), F32)]
    else:
        out_specs = [row(D_MODEL)]
        out_shape = [jax.ShapeDtypeStruct((NT, D_MODEL), F32)]
    return pl.pallas_call(
        functools.partial(_tail_kernel, last, n_h),
        grid=(NTT,),
        in_specs=[prow(D_A), prow(D_B), prow(D_C),
                  pl.BlockSpec((None, TM, D_PLE), lambda i: (layer, pi(i), 0)),
                  srow(D_A), srow(D_B), srow(D_C),
                  pl.BlockSpec((None, NS, D_PLE), lambda i: (layer, 0, 0), pipeline_mode=pl.Buffered(1))]
        + _h_specs(n_h == 2)
        + [vec(D_C), _layer_spec((D_MODEL, D_MODEL), layer),
                  vec(D_MODEL), _layer_spec((D_MODEL, D_FF), layer), _layer_spec((D_MODEL, D_FF), layer),
                  _layer_spec((D_FF, D_MODEL), layer),
                  vec(D_MODEL), _layer_spec((D_MODEL, D_MODEL), layer), _layer_spec((D_PLE, D_MODEL), layer),
                  pl.BlockSpec((1, D_MODEL), lambda i: (0, 0), pipeline_mode=pl.Buffered(1))],
        out_specs=out_specs,
        out_shape=out_shape,
        compiler_params=_params("arbitrary"),
        name="tail",
    )(ya_p, yb_p, o_p, pe_p, ya_s, yb_s, o_s, pe_s, *hs,
      w["norm_c"], w["w_out"], w["norm_ffn"], w["w_ff_gate"], w["w_ff_up"], w["w_ff_down"],
      w["norm_ple"], w["w_ple_gate"], w["w_ple_proj"], norm_final)


def _block_diag(w):
    depth, g, n, _ = w.shape
    eye = jnp.eye(g, dtype=w.dtype)
    return jnp.einsum("dgij,gh->dgihj", w, eye).reshape(depth, g * n, g * n)


def kernel(x_prompt, x_sample, cache_k, cache_v, state_conv, state_rglru, state_pool, page_table,
           p_prompt, p_sample, norm_mix, w_in, conv_w, conv_b, rg_wa, rg_ba, rg_wx, rg_bx, rg_lambda,
           norm_a, pool_w, pool_scale, norm_c, sb_bias, w_out, norm_ffn, w_ff_gate, w_ff_up, w_ff_down,
           norm_ple, w_ple_gate, w_ple_proj, norm_final):
    n_pool = cache_k.shape[1]
    tmajor = lambda x: jnp.swapaxes(x, -3, -2)

    hs = (x_prompt.reshape(NP, D_MODEL), tmajor(x_sample).reshape(NS, D_MODEL))
    pe_p = p_prompt.reshape(DEPTH, NP, D_PLE)
    pe_s = tmajor(p_sample).reshape(DEPTH, NS, D_PLE)
    ckt = jnp.transpose(cache_k, (0, 1, 3, 4, 2)).reshape(DEPTH, n_pool, D_C, PAGE_SIZE)
    cvt = jnp.transpose(cache_v, (0, 1, 3, 4, 2)).reshape(DEPTH, n_pool, D_C, PAGE_SIZE)
    pt = page_table.reshape(-1)
    sconv_t = tmajor(state_conv)
    spool_t = tmajor(state_pool)

    vec = lambda x: x.reshape(DEPTH, 1, -1)
    lw = {
        "conv_w": conv_w, "conv_b": vec(conv_b),
        "rg_wa": _block_diag(rg_wa).astype(BF16), "rg_ba": vec(rg_ba),
        "rg_wx": _block_diag(rg_wx).astype(BF16), "rg_bx": vec(rg_bx),
        "rg_lambda": vec(rg_lambda), "norm_a": vec(norm_a),
        "pool_w": _block_diag(pool_w).astype(BF16), "pool_scale": vec(pool_scale),
    }
    w_in_b = w_in.astype(BF16)
    w_kvt_b = jnp.swapaxes(w_in_b[:, :, S_K:], 1, 2)
    norm_mix3 = vec(norm_mix)
    tw = {
        "norm_c": vec(norm_c), "w_out": w_out.astype(BF16), "norm_ffn": vec(norm_ffn),
        "w_ff_gate": w_ff_gate.astype(BF16), "w_ff_up": w_ff_up.astype(BF16), "w_ff_down": w_ff_down.astype(BF16),
        "norm_ple": vec(norm_ple), "w_ple_gate": w_ple_gate.astype(BF16), "w_ple_proj": w_ple_proj.astype(BF16),
    }
    norm_final2 = norm_final.reshape(1, D_MODEL)

    kt_all = jnp.zeros((DEPTH, BATCH, D_C, SEQ), F32)
    vt_all = jnp.zeros((DEPTH, BATCH, D_C, SEQ), F32)
    outs = {n: [] for n in ("ks", "vs", "cp", "cs", "rp", "rs", "pp", "ps")}
    for l in range(DEPTH):
        xag, xb, qb, vb, ktb, kt_all, vt_all, qkv_s, kt_s, vt_s = _in_proj(
            hs, norm_mix3, w_in_b, w_kvt_b, kt_all, vt_all, l)
        ya_p, yb_p, conv_p, rg_p, pool_p = _mix_prompt(xag, xb, lw, l)
        ya_s, yb_s, conv_s, rg_s, pool_s = _mix_sample(xag, xb, sconv_t, state_rglru, spool_t, lw, l)
        o_p = _attn_prompt(sb_bias[l], qb, ktb, vb)

        bmajor = lambda x: tmajor(x.reshape(DEC_SEQ, DEC_BATCH, D_C))
        q_s = bmajor(qkv_s[:, :D_C])
        o_s = _attn_decode(pt, sb_bias[l], jnp.concatenate([q_s, q_s], axis=1),
                           bmajor(qkv_s[:, D_C:2 * D_C]), bmajor(qkv_s[:, 2 * D_C:]), ckt, cvt, l)
        o_s = tmajor(o_s).reshape(NS, D_C)

        res = _tail(ya_p, yb_p, o_p, pe_p, ya_s, yb_s, o_s, pe_s, hs, tw, norm_final2, l)
        hs = (res[0],)

        outs["ks"].append(kt_s)
        outs["vs"].append(vt_s)
        outs["cp"].append(conv_p)
        outs["cs"].append(conv_s)
        outs["rp"].append(rg_p.reshape(BATCH, D_A))
        outs["rs"].append(rg_s)
        outs["pp"].append(pool_p)
        outs["ps"].append(pool_s)

    y_prompt = res[0].reshape(BATCH, SEQ, D_MODEL)
    y_sample = tmajor(res[1].reshape(DEC_SEQ, DEC_BATCH, D_MODEL))
    st = lambda n: jnp.stack(outs[n])
    kv_prompt = lambda x: jnp.transpose(x.reshape(DEPTH, BATCH, N_C_HEADS, HEAD_DIM, SEQ), (0, 1, 4, 2, 3))
    kv_sample = lambda x: jnp.transpose(x.reshape(DEPTH, DEC_SEQ, N_C_HEADS, HEAD_DIM, DEC_BATCH), (0, 4, 1, 2, 3))
    return (y_prompt, y_sample, kv_prompt(kt_all), kv_prompt(vt_all), kv_sample(st("ks")), kv_sample(st("vs")),
            st("cp"), tmajor(st("cs")), st("rp"), st("rs"), st("pp"), tmajor(st("ps")))
```

```python
import functools

import jax
import jax.numpy as jnp
from jax import lax
from jax.experimental import pallas as pl
from jax.experimental.pallas import tpu as pltpu

F32 = jnp.float32
BF16 = jnp.bfloat16

D_MODEL = 1024
BATCH = 4
SEQ = 4096
DEPTH = 4
DEC_BATCH = 128
DEC_SEQ = 4
PAST_LEN = 2048
PAGE_SIZE = 128
N_PAGES = PAST_LEN // PAGE_SIZE
HEAD_DIM = 64
D_A = 384
D_B = 256
D_C = 384
N_C_HEADS = 6
POOL_WINDOWS = (2, 4, 8, 16)
POOL_GROUP = 64
POOL_BUF = 15
CONV_WIDTH = 4
RG_C = 8.0
D_IN = 2 * D_A + D_B + 3 * D_C
D_FF = 2816
D_PLE = 256
EPS = 1e-6
SCALE = HEAD_DIM ** -0.5
LOG2E = 1.4426950408889634

NP = BATCH * SEQ
NS = DEC_BATCH * DEC_SEQ
NT = NP + NS
TM = 512
NPT = NP // TM
NTT = NT // TM
TT = 512
TQ = 512
TK = 256
LANE = 128
HEAD_PAIR = 2 * HEAD_DIM
N_PAIRS = D_C // HEAD_PAIR
KT_ROWS = 2 * HEAD_PAIR
BIAS_ROWS = 3
VMEM_LIMIT = 56 * 1024 * 1024


def _rms(x, g):
    ms = jnp.mean(x * x, axis=-1, keepdims=True)
    return x * lax.rsqrt(ms + EPS) * g


def _dot(a, b):
    return lax.dot_general(a, b, (((1,), (0,)), ((), ())), preferred_element_type=F32)


def _dot_nt(a, b):
    return lax.dot_general(a, b, (((1,), (1,)), ((), ())), preferred_element_type=F32)


def _softplus(z):
    return jnp.maximum(z, 0.0) + jnp.log(1.0 + jnp.exp2(jnp.abs(z) * (-LOG2E)))


def _hi_lo(x):
    hi = x.astype(BF16).astype(F32)
    return jnp.concatenate([hi, x - hi], axis=1)


def _suffix_ones(n):
    r = lax.broadcasted_iota(jnp.int32, (2 * n, n), 0) & (n - 1)
    c = lax.broadcasted_iota(jnp.int32, (2 * n, n), 1)
    return jnp.where(r >= c, 1.0, 0.0).astype(BF16)


def _layer_spec(shape, layer):
    nd = len(shape)
    return pl.BlockSpec((None,) + tuple(shape), lambda *_: (layer,) + (0,) * nd,
                        pipeline_mode=pl.Buffered(1))


def _params(*sem):
    return pltpu.CompilerParams(dimension_semantics=sem, vmem_limit_bytes=VMEM_LIMIT)


S_XB = 2 * D_A
S_Q = S_XB + D_B
S_K = S_Q + D_C
S_V = S_K + D_C


def _load_h(h_refs):
    if len(h_refs) == 1:
        return h_refs[0][...]
    return jnp.where(pl.program_id(0) == NPT, h_refs[1][...], h_refs[0][...])


def _h_specs(split):
    if not split:
        return [pl.BlockSpec((TM, D_MODEL), lambda i: (i, 0))]
    return [pl.BlockSpec((TM, D_MODEL), lambda i: (jnp.minimum(i, NPT - 1), 0)),
            pl.BlockSpec((NS, D_MODEL), lambda i: (0, 0), pipeline_mode=pl.Buffered(1))]


def _in_proj_kernel(n_h, *refs):
    h_refs = refs[:n_h]
    (g_ref, w_ref, wkvt_ref, kt_prev, vt_prev,
     xag_ref, xb_ref, qb_ref, vb_ref, ktb_ref, kt_ref, vt_ref, qkvs_ref, kts_ref, vts_ref) = refs[n_h:]
    del kt_prev, vt_prev
    i = pl.program_id(0)
    hn = _rms(_load_h(h_refs), g_ref[...]).astype(BF16)
    u = _dot(hn, w_ref[:, :S_K])
    v = _dot(hn, w_ref[:, S_V:])
    kvt = _dot_nt(wkvt_ref[...], hn)
    xag_ref[...] = u[:, :S_XB]
    xb_ref[...] = u[:, S_XB:S_Q]

    @pl.when(i < NPT)
    def _():
        qb_ref[...] = u[:, S_Q:].astype(BF16)
        vb_ref[...] = v.astype(BF16)
        kt = kvt[:D_C]
        kt_ref[...] = kt
        vt_ref[...] = kvt[D_C:]
        row = lax.broadcasted_iota(jnp.int32, (HEAD_PAIR, TM), 0)
        ones_rows = jnp.where(row < BIAS_ROWS, 1.0, 0.0).astype(BF16)
        for p in range(N_PAIRS):
            ktb_ref[p * KT_ROWS:p * KT_ROWS + HEAD_PAIR, :] = kt[p * HEAD_PAIR:(p + 1) * HEAD_PAIR].astype(BF16)
            ktb_ref[p * KT_ROWS + HEAD_PAIR:(p + 1) * KT_ROWS, :] = ones_rows

    @pl.when(i == NPT)
    def _():
        qkvs_ref[:, :D_C] = u[:, S_Q:]
        qkvs_ref[:, D_C:2 * D_C] = _dot(hn, w_ref[:, S_K:S_V])
        qkvs_ref[:, 2 * D_C:] = v
        for t in range(DEC_SEQ):
            kts_ref[t] = kvt[:D_C, t * DEC_BATCH:(t + 1) * DEC_BATCH]
            vts_ref[t] = kvt[D_C:, t * DEC_BATCH:(t + 1) * DEC_BATCH]


def _in_proj(hs, norm_mix, w_in_b, w_kvt_b, kt_all, vt_all, layer):
    n_h = len(hs)
    tiles_per_seq = SEQ // TM
    pi = lambda i: jnp.minimum(i, NPT - 1)
    row = lambda w: pl.BlockSpec((TM, w), lambda i: (i, 0))
    prow = lambda w: pl.BlockSpec((TM, w), lambda i: (pi(i), 0))
    full = lambda *s: pl.BlockSpec(s, lambda i: (0,) * len(s))
    kv_t = pl.BlockSpec((None, None, D_C, TM), lambda i: (layer, pi(i) // tiles_per_seq, 0, pi(i) % tiles_per_seq))
    ktb = pl.BlockSpec((None, N_PAIRS * KT_ROWS, TM), lambda i: (pi(i) // tiles_per_seq, 0, pi(i) % tiles_per_seq))
    any_spec = pl.BlockSpec(memory_space=pl.ANY)
    kv_shape = jax.ShapeDtypeStruct((DEPTH, BATCH, D_C, SEQ), F32)
    return pl.pallas_call(
        functools.partial(_in_proj_kernel, n_h),
        grid=(NTT,),
        in_specs=_h_specs(n_h == 2) + [_layer_spec((1, D_MODEL), layer), _layer_spec((D_MODEL, D_IN), layer),
                                       _layer_spec((2 * D_C, D_MODEL), layer), any_spec, any_spec],
        out_specs=[row(2 * D_A), row(D_B), prow(D_C), prow(D_C), ktb, kv_t, kv_t,
                   full(NS, 3 * D_C), full(DEC_SEQ, D_C, DEC_BATCH), full(DEC_SEQ, D_C, DEC_BATCH)],
        out_shape=[jax.ShapeDtypeStruct((NT, 2 * D_A), F32), jax.ShapeDtypeStruct((NT, D_B), F32),
                   jax.ShapeDtypeStruct((NP, D_C), BF16), jax.ShapeDtypeStruct((NP, D_C), BF16),
                   jax.ShapeDtypeStruct((BATCH, N_PAIRS * KT_ROWS, SEQ), BF16), kv_shape, kv_shape,
                   jax.ShapeDtypeStruct((NS, 3 * D_C), F32),
                   jax.ShapeDtypeStruct((DEC_SEQ, D_C, DEC_BATCH), F32),
                   jax.ShapeDtypeStruct((DEC_SEQ, D_C, DEC_BATCH), F32)],
        input_output_aliases={n_h + 3: 5, n_h + 4: 6},
        compiler_params=_params("arbitrary"),
        name="in_proj",
    )(*hs, norm_mix, w_in_b, w_kvt_b, kt_all, vt_all)


def _rg_gates(xc, wa, ba, wx, bx, lam):
    xcb = xc.astype(BF16)
    r = jax.nn.sigmoid(_dot(xcb, wa) + ba)
    ig = jax.nn.sigmoid(_dot(xcb, wx) + bx)
    log_a = RG_C * r * jax.nn.log_sigmoid(lam)
    a = jnp.exp(log_a)
    om = 1.0 - a * a
    u = jnp.where(om > 0.0, om * lax.rsqrt(om), 0.0) * (ig * xc)
    return a, u


def _pool_lane_select(lane, per_window):
    out = per_window[-1]
    for g in range(len(POOL_WINDOWS) - 2, -1, -1):
        out = jnp.where(lane < (g + 1) * POOL_GROUP, per_window[g], out)
    return out


def _mix_prompt_kernel(xag_ref, xb_ref, cw_ref, cb_ref, wa_ref, ba_ref, wx_ref, bx_ref, lam_ref,
                       na_ref, pw_ref, ps_ref,
                       ya_ref, yb_ref, convn_ref, rgn_ref, pooln_ref,
                       xbuf, pbuf, hcar, hs_s):
    j = pl.program_id(1)
    hist_a = 8
    hist_b = 16

    @pl.when(j == 0)
    def _():
        xbuf[0:hist_a, :] = jnp.zeros((hist_a, D_A), F32)
        pbuf[0:hist_b, :] = jnp.zeros((hist_b, D_B), F32)
        hcar[...] = jnp.zeros_like(hcar)

    xa = xag_ref[:, 0:D_A]
    ga = xag_ref[:, D_A:2 * D_A]
    xbuf[hist_a:hist_a + TT, :] = xa
    cw = cw_ref[...]
    assert CONV_WIDTH == 4
    e = xbuf[...]
    e1 = pltpu.roll(e, 1, 0)
    p2 = pltpu.roll(cw[1:2, :] * e + cw[0:1, :] * e1, 2, 0)
    xc = (cb_ref[...] + cw[3:4, :] * e + cw[2:3, :] * e1 + p2)[hist_a:hist_a + TT]
    convn_ref[0] = xbuf[hist_a + TT - (CONV_WIDTH - 1):hist_a + TT, :]
    xbuf[0:hist_a, :] = xbuf[TT:TT + hist_a, :]

    a, u = _rg_gates(xc, wa_ref[...], ba_ref[...], wx_ref[...], bx_ref[...], lam_ref[...])
    sub_rows = 8
    a3 = a.reshape(TT // sub_rows, sub_rows, D_A)
    u3 = u.reshape(TT // sub_rows, sub_rows, D_A)
    sub = lax.broadcasted_iota(jnp.int32, a3.shape, 1)
    d = 1
    while d < sub_rows:
        m = sub >= d
        a_s = jnp.where(m, pltpu.roll(a3, d, 1), 1.0)
        u_s = jnp.where(m, pltpu.roll(u3, d, 1), 0.0)
        u3 = a3 * u_s + u3
        a3 = a3 * a_s
        d *= 2
    h = hcar[0:1, :]
    for g in range(TT // sub_rows):
        hs_g = u3[g] + a3[g] * h
        hs_s[g * sub_rows:(g + 1) * sub_rows, :] = hs_g
        h = hs_g[sub_rows - 1:sub_rows, :]
    hcar[0:1, :] = h
    rgn_ref[0] = h
    hs = hs_s[...]
    ya_ref[...] = _rms(jax.nn.gelu(ga) * hs, na_ref[...])

    xb = xb_ref[...]
    pbuf[hist_b:hist_b + TT, :] = xb
    lane = lax.broadcasted_iota(jnp.int32, (TT, D_B), 1)
    pos = j * TT + lax.broadcasted_iota(jnp.int32, (TT, D_B), 0)
    assert POOL_WINDOWS == (2, 4, 8, 16) and D_B == 2 * LANE and 2 * POOL_GROUP == LANE
    e = pbuf[...]
    s2 = e + pltpu.roll(e, 1, 0)
    s4 = s2 + pltpu.roll(s2, 2, 0)
    s4_hi = s4[:, LANE:]
    s8_hi = s4_hi + pltpu.roll(s4_hi, 4, 0)
    s16_hi = s8_hi + pltpu.roll(s8_hi, 8, 0)
    tile = lambda x: x[hist_b:hist_b + TT]
    lane_h = lax.broadcasted_iota(jnp.int32, (TT, LANE), 1)
    wsum = jnp.concatenate(
        [jnp.where(lane_h < POOL_GROUP, tile(s2[:, :LANE]), tile(s4[:, :LANE])),
         jnp.where(lane_h < POOL_GROUP, tile(s8_hi), tile(s16_hi))], axis=1)
    win = _pool_lane_select(lane, [jnp.full((TT, D_B), w, jnp.int32) for w in POOL_WINDOWS])
    cnt = jnp.minimum(win, pos + 1).astype(F32)
    pooled = (wsum / cnt - xb).astype(BF16)
    yb_ref[...] = _dot(pooled, pw_ref[...]) * ps_ref[...]
    pooln_ref[0] = pbuf[hist_b + TT - POOL_BUF:hist_b + TT, :]
    pbuf[0:hist_b, :] = pbuf[TT:TT + hist_b, :]


def _mix_prompt(xag, xb, lw, layer):
    nj = SEQ // TT
    row = lambda w: pl.BlockSpec((TT, w), lambda b, j: (b * nj + j, 0))
    vec = lambda w: _layer_spec((1, w), layer)
    state = lambda r, w: pl.BlockSpec((1, r, w), lambda b, j: (b, 0, 0))
    return pl.pallas_call(
        _mix_prompt_kernel,
        grid=(BATCH, nj),
        in_specs=[row(2 * D_A), row(D_B), _layer_spec((CONV_WIDTH, D_A), layer), vec(D_A),
                  _layer_spec((D_A, D_A), layer), vec(D_A), _layer_spec((D_A, D_A), layer), vec(D_A),
                  vec(D_A), vec(D_A), _layer_spec((D_B, D_B), layer), vec(D_B)],
        out_specs=[row(D_A), row(D_B), state(CONV_WIDTH - 1, D_A), state(1, D_A), state(POOL_BUF, D_B)],
        out_shape=[jax.ShapeDtypeStruct((NP, D_A), F32), jax.ShapeDtypeStruct((NP, D_B), F32),
                   jax.ShapeDtypeStruct((BATCH, CONV_WIDTH - 1, D_A), F32),
                   jax.ShapeDtypeStruct((BATCH, 1, D_A), F32),
                   jax.ShapeDtypeStruct((BATCH, POOL_BUF, D_B), F32)],
        scratch_shapes=[pltpu.VMEM((TT + 8, D_A), F32), pltpu.VMEM((TT + 16, D_B), F32),
                        pltpu.VMEM((8, D_A), F32), pltpu.VMEM((TT, D_A), F32)],
        compiler_params=_params("parallel", "arbitrary"),
        name="mix_prompt",
    )(xag, xb, lw["conv_w"], lw["conv_b"], lw["rg_wa"], lw["rg_ba"], lw["rg_wx"], lw["rg_bx"],
      lw["rg_lambda"], lw["norm_a"], lw["pool_w"], lw["pool_scale"])


def _mix_sample_kernel(xag_ref, xb_ref, sconv_ref, srg_ref, spool_ref,
                       cw_ref, cb_ref, wa_ref, ba_ref, wx_ref, bx_ref, lam_ref, na_ref, pw_ref, ps_ref,
                       ya_ref, yb_ref, convn_ref, rgn_ref, pooln_ref):
    nb = DEC_BATCH
    rows = lambda t: slice(t * nb, (t + 1) * nb)
    xe = [sconv_ref[0, t] for t in range(CONV_WIDTH - 1)] + [xag_ref[rows(t), 0:D_A] for t in range(DEC_SEQ)]
    cw = cw_ref[...]
    xcs = []
    for t in range(DEC_SEQ):
        y = cb_ref[...]
        for jj in range(CONV_WIDTH):
            y = y + cw[jj:jj + 1, :] * xe[t + jj]
        xcs.append(y)
    for t in range(CONV_WIDTH - 1):
        convn_ref[t] = xe[len(xe) - (CONV_WIDTH - 1) + t]
    xc = jnp.concatenate(xcs, axis=0)
    a, u = _rg_gates(xc, wa_ref[...], ba_ref[...], wx_ref[...], bx_ref[...], lam_ref[...])
    h = srg_ref[0]
    hs = []
    for t in range(DEC_SEQ):
        h = a[rows(t)] * h + u[rows(t)]
        hs.append(h)
    rgn_ref[...] = h
    ga = xag_ref[:, D_A:2 * D_A]
    ya_ref[...] = _rms(jax.nn.gelu(ga) * jnp.concatenate(hs, axis=0), na_ref[...])

    pe = [spool_ref[0, t] for t in range(POOL_BUF)] + [xb_ref[rows(t), :] for t in range(DEC_SEQ)]
    for t in range(POOL_BUF):
        pooln_ref[t] = pe[len(pe) - POOL_BUF + t]
    lane = lax.broadcasted_iota(jnp.int32, (nb, D_B), 1)
    pooled = []
    for t in range(DEC_SEQ):
        s = pe[POOL_BUF + t]
        means = []
        for lag in range(1, max(POOL_WINDOWS)):
            s = s + pe[POOL_BUF + t - lag]
            if lag + 1 in POOL_WINDOWS:
                means.append(s / float(min(lag + 1, PAST_LEN + t + 1)))
        pooled.append(_pool_lane_select(lane, means) - pe[POOL_BUF + t])
    pooled = jnp.concatenate(pooled, axis=0).astype(BF16)
    yb_ref[...] = _dot(pooled, pw_ref[...]) * ps_ref[...]


def _mix_sample(xag, xb, sconv_t, srg, spool_t, lw, layer):
    vec = lambda w: _layer_spec((1, w), layer)
    lay = lambda *s: pl.BlockSpec((1,) + s, lambda i: (layer,) + (0,) * len(s))
    full = lambda *s: pl.BlockSpec(s, lambda i: (0,) * len(s))
    return pl.pallas_call(
        _mix_sample_kernel,
        grid=(1,),
        in_specs=[pl.BlockSpec((NS, 2 * D_A), lambda i: (NPT, 0)), pl.BlockSpec((NS, D_B), lambda i: (NPT, 0)),
                  lay(CONV_WIDTH - 1, DEC_BATCH, D_A), lay(DEC_BATCH, D_A), lay(POOL_BUF, DEC_BATCH, D_B),
                  _layer_spec((CONV_WIDTH, D_A), layer), vec(D_A),
                  _layer_spec((D_A, D_A), layer), vec(D_A), _layer_spec((D_A, D_A), layer), vec(D_A),
                  vec(D_A), vec(D_A), _layer_spec((D_B, D_B), layer), vec(D_B)],
        out_specs=[full(NS, D_A), full(NS, D_B), full(CONV_WIDTH - 1, DEC_BATCH, D_A), full(DEC_BATCH, D_A),
                   full(POOL_BUF, DEC_BATCH, D_B)],
        out_shape=[jax.ShapeDtypeStruct((NS, D_A), F32), jax.ShapeDtypeStruct((NS, D_B), F32),
                   jax.ShapeDtypeStruct((CONV_WIDTH - 1, DEC_BATCH, D_A), F32),
                   jax.ShapeDtypeStruct((DEC_BATCH, D_A), F32),
                   jax.ShapeDtypeStruct((POOL_BUF, DEC_BATCH, D_B), F32)],
        compiler_params=_params("arbitrary"),
        name="mix_sample",
    )(xag, xb, sconv_t, srg, spool_t, lw["conv_w"], lw["conv_b"], lw["rg_wa"], lw["rg_ba"], lw["rg_wx"],
      lw["rg_bx"], lw["rg_lambda"], lw["norm_a"], lw["pool_w"], lw["pool_scale"])


def _attn_prompt_kernel(bias_ref, q_ref, kt_ref, v_ref, o_ref, za_s, zb_s, acc_s, run_s):
    assert TQ == 2 * TK
    p = pl.program_id(1)
    i = pl.program_id(2)
    q = q_ref[...]
    lane_q = lax.broadcasted_iota(jnp.int32, (TQ, HEAD_PAIR), 1)
    qs = jnp.concatenate([jnp.where(lane_q < HEAD_DIM, q, jnp.zeros_like(q)),
                          jnp.where(lane_q >= HEAD_DIM, q, jnp.zeros_like(q))], axis=0) * jnp.asarray(SCALE, BF16)

    def offset_lanes(bias):
        b = jnp.full((TQ, HEAD_PAIR), bias, F32)
        terms = []
        for _ in range(BIAS_ROWS):
            terms.append(b.astype(BF16).astype(F32))
            b = b - terms[-1]
        out = jnp.zeros((TQ, HEAD_PAIR), F32)
        for n in range(BIAS_ROWS - 1, -1, -1):
            out = jnp.where(lane_q == n, terms[n], out)
        return out.astype(BF16)

    qs = jnp.concatenate([qs, jnp.concatenate([offset_lanes(bias_ref[2 * p]), offset_lanes(bias_ref[2 * p + 1])],
                                              axis=0)], axis=1)
    r = lax.broadcasted_iota(jnp.int32, (2 * TQ, TK), 0) & (TQ - 1)
    c = lax.broadcasted_iota(jnp.int32, (2 * TQ, TK), 1)
    ones = _suffix_ones(TK)

    def scores(j):
        start = pl.multiple_of(j * TK, TK)
        return _dot(qs, kt_ref[:, pl.ds(start, TK)])

    def process(z_ref, j, diag_off):
        z = z_ref[...]
        start = pl.multiple_of(j * TK, TK)
        sp = _softplus(z)
        if diag_off is not None:
            causal = c + diag_off * TK < r
            sp = jnp.where(causal, sp, 0.0)
        run = run_s[...]
        cs = _dot(_hi_lo(sp), ones)
        a = jnp.exp(z - (cs + jnp.concatenate([run] * (TK // LANE), axis=1)))
        if diag_off is not None:
            a = jnp.where(causal, a, 0.0)
        acc_s[...] += _dot(a, v_ref[pl.ds(start, TK), :])
        run_s[...] = run + jnp.broadcast_to(cs[:, 0:1], run.shape)

    def process_last_tile(j):
        lower = lambda x: jnp.concatenate([x[TK:TQ], x[TQ + TK:]], axis=0)
        start = pl.multiple_of(j * TK, TK)
        z = _dot(lower(qs), kt_ref[:, pl.ds(start, TK)])
        rows = lax.broadcasted_iota(jnp.int32, (TQ, TK), 0) & (TK - 1)
        causal = lax.broadcasted_iota(jnp.int32, (TQ, TK), 1) < rows
        sp = jnp.where(causal, _softplus(z), 0.0)
        cs = _dot(_hi_lo(sp), ones)
        a = jnp.where(causal, jnp.exp(z - cs), 0.0)
        av = _dot(a, v_ref[pl.ds(start, TK), :])
        tot = jnp.broadcast_to(cs[:, 0:1], (TQ, HEAD_PAIR))
        zeros = jnp.zeros((TK, HEAD_PAIR), F32)
        acc_s[...] = jnp.concatenate([zeros, av[:TK], zeros, av[TK:]], axis=0)
        run_s[...] = jnp.concatenate([zeros, tot[:TK], zeros, tot[TK:]], axis=0)

    t0 = 2 * i + 1
    zb_s[...] = scores(t0 - 1)
    process_last_tile(t0)
    za_s[...] = scores(jnp.maximum(t0 - 2, 0))
    process(zb_s, t0 - 1, 0)

    def two_tiles(jj, carry):
        t = t0 - 2 - 2 * jj
        zb_s[...] = scores(t - 1)
        process(za_s, t, None)
        za_s[...] = scores(jnp.maximum(t - 2, 0))
        process(zb_s, t - 1, None)
        return carry

    lax.fori_loop(0, i, two_tiles, 0)
    acc = acc_s[...]
    o_ref[...] = jnp.where(lane_q < HEAD_DIM, acc[:TQ], acc[TQ:])


def _attn_prompt(bias, qb, ktb, vb):
    nq = SEQ // TQ
    return pl.pallas_call(
        _attn_prompt_kernel,
        grid=(BATCH, N_PAIRS, nq),
        in_specs=[pl.BlockSpec(memory_space=pltpu.SMEM),
                  pl.BlockSpec((TQ, HEAD_PAIR), lambda b, p, i: (b * nq + i, p)),
                  pl.BlockSpec((None, KT_ROWS, SEQ), lambda b, p, i: (b, p, 0)),
                  pl.BlockSpec((SEQ, HEAD_PAIR), lambda b, p, i: (b, p))],
        out_specs=pl.BlockSpec((TQ, HEAD_PAIR), lambda b, p, i: (b * nq + i, p)),
        out_shape=jax.ShapeDtypeStruct((NP, D_C), F32),
        scratch_shapes=[pltpu.VMEM((2 * TQ, TK), F32), pltpu.VMEM((2 * TQ, TK), F32),
                        pltpu.VMEM((2 * TQ, HEAD_PAIR), F32), pltpu.VMEM((2 * TQ, HEAD_PAIR), F32)],
        compiler_params=_params("parallel", "parallel", "arbitrary"),
        name="attn_prompt",
    )(bias, qb, ktb, vb)


QROWS = 32
N_BLOCKS = N_PAGES + 1


def _attn_decode_kernel(layer, pt_ref, bias_ref, q8_ref, kn_ref, vn_ref, ckt_hbm, cvt_hbm, o_ref,
                        kbuf, vbuf, kn_s, vn_s, sem):
    b = pl.program_id(0)
    slot = b & 1

    def page_copies(seq, slot):
        cps = []
        for pg in range(N_PAGES):
            page = pt_ref[seq * N_PAGES + pg]
            lanes = pl.ds(pg * PAGE_SIZE, PAGE_SIZE)
            cps.append(pltpu.make_async_copy(ckt_hbm.at[layer, page], kbuf.at[slot, :, lanes], sem.at[slot]))
            cps.append(pltpu.make_async_copy(cvt_hbm.at[layer, page], vbuf.at[slot, :, lanes], sem.at[slot]))
        return cps

    @pl.when(b == 0)
    def _():
        kn_s[...] = jnp.zeros_like(kn_s)
        vn_s[...] = jnp.zeros_like(vn_s)
        for cp in page_copies(0, 0):
            cp.start()

    @pl.when(b + 1 < DEC_BATCH)
    def _():
        for cp in page_copies(b + 1, 1 - slot):
            cp.start()

    kn_s[0:DEC_SEQ, :] = kn_ref[0]
    vn_s[0:DEC_SEQ, :] = vn_ref[0]

    rq = lax.broadcasted_iota(jnp.int32, (QROWS, D_C), 0)
    cq = lax.broadcasted_iota(jnp.int32, (QROWS, D_C), 1)
    head_lanes = (cq >> 6) == (rq >> 2)
    q8 = q8_ref[0] * SCALE
    qbd = jnp.where(head_lanes, jnp.concatenate([q8] * (QROWS // 8), axis=0), 0.0)

    rr = lax.broadcasted_iota(jnp.int32, (QROWS, PAGE_SIZE), 0)
    cc = lax.broadcasted_iota(jnp.int32, (QROWS, PAGE_SIZE), 1)
    bias = jnp.zeros((QROWS, PAGE_SIZE), F32)
    for h in range(N_C_HEADS):
        bias = jnp.where((rr >> 2) == h, bias_ref[h], bias)
    new_valid = cc < (rr & 3)
    z_new = _dot_nt(qbd, kn_s[...]) + bias

    for cp in page_copies(b, slot):
        cp.wait()

    z_wide = _dot(qbd, kbuf[slot])
    z_old = jnp.concatenate([z_wide[:, n * PAGE_SIZE:(n + 1) * PAGE_SIZE] + bias for n in range(N_PAGES)], axis=0)
    sp = jnp.concatenate([_softplus(z_old), jnp.where(new_valid, _softplus(z_new), 0.0)], axis=0)
    z = jnp.concatenate([z_old, z_new], axis=0)
    cs = _dot(_hi_lo(sp), _suffix_ones(PAGE_SIZE))

    run = jnp.zeros((QROWS, 1), F32)
    runs = [None] * N_BLOCKS
    for n in range(N_BLOCKS - 1, -1, -1):
        runs[n] = jnp.broadcast_to(run, (QROWS, PAGE_SIZE))
        run = run + cs[n * QROWS:(n + 1) * QROWS, 0:1]
    a = jnp.exp(z - (cs + jnp.concatenate(runs, axis=0)))
    blk = lambda n: a[n * QROWS:(n + 1) * QROWS]

    a_wide = jnp.concatenate([blk(n) for n in range(N_PAGES)], axis=1)
    acc = _dot(jnp.where(new_valid, blk(N_PAGES), 0.0), vn_s[...]) + _dot_nt(a_wide, vbuf[slot])

    acc = jnp.where(head_lanes, acc, 0.0)
    s = acc[0:8] + acc[8:16] + acc[16:24]
    s = s + pltpu.roll(s, 4, 0)
    o_ref[0] = s[0:DEC_SEQ]


def _attn_decode(page_table_flat, bias, q8, kn, vn, cache_kt, cache_vt, layer):
    seq = lambda r: pl.BlockSpec((1, r, D_C), lambda b, pt: (b, 0, 0))
    hbm = pl.BlockSpec(memory_space=pl.ANY)
    grid_spec = pltpu.PrefetchScalarGridSpec(
        num_scalar_prefetch=1,
        grid=(DEC_BATCH,),
        in_specs=[pl.BlockSpec(memory_space=pltpu.SMEM), seq(8), seq(DEC_SEQ), seq(DEC_SEQ), hbm, hbm],
        out_specs=seq(DEC_SEQ),
        scratch_shapes=[pltpu.VMEM((2, D_C, PAST_LEN), F32), pltpu.VMEM((2, D_C, PAST_LEN), F32),
                        pltpu.VMEM((PAGE_SIZE, D_C), F32), pltpu.VMEM((PAGE_SIZE, D_C), F32),
                        pltpu.SemaphoreType.DMA((2,))],
    )
    return pl.pallas_call(
        functools.partial(_attn_decode_kernel, layer),
        grid_spec=grid_spec,
        out_shape=jax.ShapeDtypeStruct((DEC_BATCH, DEC_SEQ, D_C), F32),
        compiler_params=_params("arbitrary"),
        name="attn_decode",
    )(page_table_flat, bias, q8, kn, vn, cache_kt, cache_vt)


MXU_TILE = 256
FF_SPLIT = (D_FF // MXU_TILE + 1) // 2 * MXU_TILE


def _tail_kernel(last, n_h, yap_ref, ybp_ref, op_ref, pep_ref, yas_ref, ybs_ref, os_ref, pes_ref, *refs):
    h_refs = refs[:n_h]
    nc_ref, wout_ref, nf_ref, wg_ref, wu_ref, wd_ref, np_ref, wpg_ref, wpp_ref, nfin_ref = refs[n_h:n_h + 10]
    out_refs = refs[n_h + 10:]
    i = pl.program_id(0)
    is_sample = i == NPT
    pick = lambda s_ref, p_ref: jnp.where(is_sample, s_ref[...], p_ref[...])
    yc = _rms(pick(os_ref, op_ref), nc_ref[...])
    y = jnp.concatenate([pick(yas_ref, yap_ref), pick(ybs_ref, ybp_ref), yc], axis=-1).astype(BF16)
    h = _load_h(h_refs) + _dot(y, wout_ref[...])

    hn = _rms(h, nf_ref[...]).astype(BF16)
    for cols in (slice(0, FF_SPLIT), slice(FF_SPLIT, D_FF)):
        act = (jax.nn.silu(_dot(hn, wg_ref[:, cols])) * _dot(hn, wu_ref[:, cols])).astype(BF16)
        h = h + _dot(act, wd_ref[cols, :])

    hn = _rms(h, np_ref[...]).astype(BF16)
    gate = jax.nn.sigmoid(_dot(hn, wpg_ref[...]))
    h = h + gate * _dot(pick(pes_ref, pep_ref).astype(BF16), wpp_ref[...])

    if not last:
        out_refs[0][...] = h
    else:
        y_out = _rms(h, nfin_ref[...])
        yp_ref, ys_ref = out_refs

        @pl.when(i < NPT)
        def _():
            yp_ref[...] = y_out

        @pl.when(is_sample)
        def _():
            ys_ref[...] = y_out


def _tail(ya_p, yb_p, o_p, pe_p, ya_s, yb_s, o_s, pe_s, hs, w, norm_final, layer):
    last = layer == DEPTH - 1
    n_h = len(hs)
    pi = lambda i: jnp.minimum(i, NPT - 1)
    prow = lambda wd: pl.BlockSpec((TM, wd), lambda i: (pi(i), 0))
    srow = lambda wd: pl.BlockSpec((NS, wd), lambda i: (0, 0), pipeline_mode=pl.Buffered(1))
    row = lambda wd: pl.BlockSpec((TM, wd), lambda i: (i, 0))
    vec = lambda wd: _layer_spec((1, wd), layer)
    if last:
        out_specs = [prow(D_MODEL), pl.BlockSpec((NS, D_MODEL), lambda i: (0, 0))]
        out_shape = [jax.ShapeDtypeStruct((NP, D_MODEL), F32), jax.ShapeDtypeStruct((NS, D_MODEL), F32)]
    else:
        out_specs = [row(D_MODEL)]
        out_shape = [jax.ShapeDtypeStruct((NT, D_MODEL), F32)]
    return pl.pallas_call(
        functools.partial(_tail_kernel, last, n_h),
        grid=(NTT,),
        in_specs=[prow(D_A), prow(D_B), prow(D_C),
                  pl.BlockSpec((None, TM, D_PLE), lambda i: (layer, pi(i), 0)),
                  srow(D_A), srow(D_B), srow(D_C),
                  pl.BlockSpec((None, NS, D_PLE), lambda i: (layer, 0, 0), pipeline_mode=pl.Buffered(1))]
        + _h_specs(n_h == 2)
        + [vec(D_C), _layer_spec((D_MODEL, D_MODEL), layer),
                  vec(D_MODEL), _layer_spec((D_MODEL, D_FF), layer), _layer_spec((D_MODEL, D_FF), layer),
                  _layer_spec((D_FF, D_MODEL), layer),
                  vec(D_MODEL), _layer_spec((D_MODEL, D_MODEL), layer), _layer_spec((D_PLE, D_MODEL), layer),
                  pl.BlockSpec((1, D_MODEL), lambda i: (0, 0), pipeline_mode=pl.Buffered(1))],
        out_specs=out_specs,
        out_shape=out_shape,
        compiler_params=_params("arbitrary"),
        name="tail",
    )(ya_p, yb_p, o_p, pe_p, ya_s, yb_s, o_s, pe_s, *hs,
      w["norm_c"], w["w_out"], w["norm_ffn"], w["w_ff_gate"], w["w_ff_up"], w["w_ff_down"],
      w["norm_ple"], w["w_ple_gate"], w["w_ple_proj"], norm_final)


def _block_diag(w):
    depth, g, n, _ = w.shape
    eye = jnp.eye(g, dtype=w.dtype)
    return jnp.einsum("dgij,gh->dgihj", w, eye).reshape(depth, g * n, g * n)


def kernel(x_prompt, x_sample, cache_k, cache_v, state_conv, state_rglru, state_pool, page_table,
           p_prompt, p_sample, norm_mix, w_in, conv_w, conv_b, rg_wa, rg_ba, rg_wx, rg_bx, rg_lambda,
           norm_a, pool_w, pool_scale, norm_c, sb_bias, w_out, norm_ffn, w_ff_gate, w_ff_up, w_ff_down,
           norm_ple, w_ple_gate, w_ple_proj, norm_final):
    n_pool = cache_k.shape[1]
    tmajor = lambda x: jnp.swapaxes(x, -3, -2)

    hs = (x_prompt.reshape(NP, D_MODEL), tmajor(x_sample).reshape(NS, D_MODEL))
    pe_p = p_prompt.reshape(DEPTH, NP, D_PLE)
    pe_s = tmajor(p_sample).reshape(DEPTH, NS, D_PLE)
    ckt = jnp.transpose(cache_k, (0, 1, 3, 4, 2)).reshape(DEPTH, n_pool, D_C, PAGE_SIZE)
    cvt = jnp.transpose(cache_v, (0, 1, 3, 4, 2)).reshape(DEPTH, n_pool, D_C, PAGE_SIZE)
    pt = page_table.reshape(-1)
    sconv_t = tmajor(state_conv)
    spool_t = tmajor(state_pool)

    vec = lambda x: x.reshape(DEPTH, 1, -1)
    lw = {
        "conv_w": conv_w, "conv_b": vec(conv_b),
        "rg_wa": _block_diag(rg_wa).astype(BF16), "rg_ba": vec(rg_ba),
        "rg_wx": _block_diag(rg_wx).astype(BF16), "rg_bx": vec(rg_bx),
        "rg_lambda": vec(rg_lambda), "norm_a": vec(norm_a),
        "pool_w": _block_diag(pool_w).astype(BF16), "pool_scale": vec(pool_scale),
    }
    w_in_b = w_in.astype(BF16)
    w_kvt_b = jnp.swapaxes(lax.optimization_barrier(w_in[:, :, S_K:]), 1, 2).astype(BF16)
    norm_mix3 = vec(norm_mix)
    tw = {
        "norm_c": vec(norm_c), "w_out": w_out.astype(BF16), "norm_ffn": vec(norm_ffn),
        "w_ff_gate": w_ff_gate.astype(BF16), "w_ff_up": w_ff_up.astype(BF16), "w_ff_down": w_ff_down.astype(BF16),
        "norm_ple": vec(norm_ple), "w_ple_gate": w_ple_gate.astype(BF16), "w_ple_proj": w_ple_proj.astype(BF16),
    }
    norm_final2 = norm_final.reshape(1, D_MODEL)

    kt_all = jnp.zeros((DEPTH, BATCH, D_C, SEQ), F32)
    vt_all = jnp.zeros((DEPTH, BATCH, D_C, SEQ), F32)
    outs = {n: [] for n in ("ks", "vs", "cp", "cs", "rp", "rs", "pp", "ps")}
    for l in range(DEPTH):
        xag, xb, qb, vb, ktb, kt_all, vt_all, qkv_s, kt_s, vt_s = _in_proj(
            hs, norm_mix3, w_in_b, w_kvt_b, kt_all, vt_all, l)
        ya_p, yb_p, conv_p, rg_p, pool_p = _mix_prompt(xag, xb, lw, l)
        ya_s, yb_s, conv_s, rg_s, pool_s = _mix_sample(xag, xb, sconv_t, state_rglru, spool_t, lw, l)
        o_p = _attn_prompt(sb_bias[l], qb, ktb, vb)

        bmajor = lambda x: tmajor(x.reshape(DEC_SEQ, DEC_BATCH, D_C))
        q_s = bmajor(qkv_s[:, :D_C])
        o_s = _attn_decode(pt, sb_bias[l], jnp.concatenate([q_s, q_s], axis=1),
                           bmajor(qkv_s[:, D_C:2 * D_C]), bmajor(qkv_s[:, 2 * D_C:]), ckt, cvt, l)
        o_s = tmajor(o_s).reshape(NS, D_C)

        res = _tail(ya_p, yb_p, o_p, pe_p, ya_s, yb_s, o_s, pe_s, hs, tw, norm_final2, l)
        hs = (res[0],)

        outs["ks"].append(kt_s)
        outs["vs"].append(vt_s)
        outs["cp"].append(conv_p)
        outs["cs"].append(conv_s)
        outs["rp"].append(rg_p.reshape(BATCH, D_A))
        outs["rs"].append(rg_s)
        outs["pp"].append(pool_p)
        outs["ps"].append(pool_s)

    y_prompt = res[0].reshape(BATCH, SEQ, D_MODEL)
    y_sample = tmajor(res[1].reshape(DEC_SEQ, DEC_BATCH, D_MODEL))
    st = lambda n: jnp.stack(outs[n])
    kv_prompt = lambda x: jnp.transpose(x.reshape(DEPTH, BATCH, N_C_HEADS, HEAD_DIM, SEQ), (0, 1, 4, 2, 3))
    kv_sample = lambda x: jnp.transpose(x.reshape(DEPTH, DEC_SEQ, N_C_HEADS, HEAD_DIM, DEC_BATCH), (0, 4, 1, 2, 3))
    return (y_prompt, y_sample, kv_prompt(kt_all), kv_prompt(vt_all), kv_sample(st("ks")), kv_sample(st("vs")),
            st("cp"), tmajor(st("cs")), st("rp"), st("rs"), st("pp"), tmajor(st("ps")))
```

```python
import functools

import jax
import jax.numpy as jnp
from jax import lax
from jax.experimental import pallas as pl
from jax.experimental.pallas import tpu as pltpu

F32 = jnp.float32
BF16 = jnp.bfloat16

D_MODEL = 1024
BATCH = 4
SEQ = 4096
DEPTH = 4
DEC_BATCH = 128
DEC_SEQ = 4
PAST_LEN = 2048
PAGE_SIZE = 128
N_PAGES = PAST_LEN // PAGE_SIZE
HEAD_DIM = 64
D_A = 384
D_B = 256
D_C = 384
N_C_HEADS = 6
POOL_WINDOWS = (2, 4, 8, 16)
POOL_GROUP = 64
POOL_BUF = 15
CONV_WIDTH = 4
RG_C = 8.0
D_IN = 2 * D_A + D_B + 3 * D_C
D_FF = 2816
D_PLE = 256
EPS = 1e-6
SCALE = HEAD_DIM ** -0.5
LOG2E = 1.4426950408889634

NP = BATCH * SEQ
NS = DEC_BATCH * DEC_SEQ
NT = NP + NS
TM = 512
NPT = NP // TM
NTT = NT // TM
TT = 512
TQ = 512
TK = 256
LANE = 128
HEAD_PAIR = 2 * HEAD_DIM
N_PAIRS = D_C // HEAD_PAIR
KT_ROWS = 2 * HEAD_PAIR
BIAS_ROWS = 3
VMEM_LIMIT = 56 * 1024 * 1024


def _rms(x, g):
    ms = jnp.mean(x * x, axis=-1, keepdims=True)
    return x * lax.rsqrt(ms + EPS) * g


def _dot(a, b):
    return lax.dot_general(a, b, (((1,), (0,)), ((), ())), preferred_element_type=F32)


def _dot_nt(a, b):
    return lax.dot_general(a, b, (((1,), (1,)), ((), ())), preferred_element_type=F32)


def _softplus(z):
    return jnp.maximum(z, 0.0) + jnp.log(1.0 + jnp.exp2(jnp.abs(z) * (-LOG2E)))


def _hi_lo(x):
    hi = x.astype(BF16).astype(F32)
    return jnp.concatenate([hi, x - hi], axis=1)


def _suffix_ones(n):
    r = lax.broadcasted_iota(jnp.int32, (2 * n, n), 0) & (n - 1)
    c = lax.broadcasted_iota(jnp.int32, (2 * n, n), 1)
    return jnp.where(r >= c, 1.0, 0.0).astype(BF16)


def _layer_spec(shape, layer):
    nd = len(shape)
    return pl.BlockSpec((None,) + tuple(shape), lambda *_: (layer,) + (0,) * nd,
                        pipeline_mode=pl.Buffered(1))


def _params(*sem):
    return pltpu.CompilerParams(dimension_semantics=sem, vmem_limit_bytes=VMEM_LIMIT)


S_XB = 2 * D_A
S_Q = S_XB + D_B
S_K = S_Q + D_C
S_V = S_K + D_C


def _load_h(h_refs):
    if len(h_refs) == 1:
        return h_refs[0][...]
    return jnp.where(pl.program_id(0) == NPT, h_refs[1][...], h_refs[0][...])


def _h_specs(split):
    if not split:
        return [pl.BlockSpec((TM, D_MODEL), lambda i: (i, 0))]
    return [pl.BlockSpec((TM, D_MODEL), lambda i: (jnp.minimum(i, NPT - 1), 0)),
            pl.BlockSpec((NS, D_MODEL), lambda i: (0, 0), pipeline_mode=pl.Buffered(1))]


def _in_proj_kernel(n_h, *refs):
    h_refs = refs[:n_h]
    (g_ref, w_ref, wkvt_ref, kt_prev, vt_prev,
     xag_ref, xb_ref, qb_ref, vb_ref, ktb_ref, kt_ref, vt_ref, qkvs_ref, kts_ref, vts_ref) = refs[n_h:]
    del kt_prev, vt_prev
    i = pl.program_id(0)
    hn = _rms(_load_h(h_refs), g_ref[...]).astype(BF16)
    u = _dot(hn, w_ref[:, :S_K])
    v = _dot(hn, w_ref[:, S_V:])
    kvt = _dot_nt(wkvt_ref[...], hn)
    xag_ref[...] = u[:, :S_XB]
    xb_ref[...] = u[:, S_XB:S_Q]

    @pl.when(i < NPT)
    def _():
        qb_ref[...] = u[:, S_Q:].astype(BF16)
        vb_ref[...] = v.astype(BF16)
        kt = kvt[:D_C]
        kt_ref[...] = kt
        vt_ref[...] = kvt[D_C:]
        row = lax.broadcasted_iota(jnp.int32, (HEAD_PAIR, TM), 0)
        ones_rows = jnp.where(row < BIAS_ROWS, 1.0, 0.0).astype(BF16)
        for p in range(N_PAIRS):
            ktb_ref[p * KT_ROWS:p * KT_ROWS + HEAD_PAIR, :] = kt[p * HEAD_PAIR:(p + 1) * HEAD_PAIR].astype(BF16)
            ktb_ref[p * KT_ROWS + HEAD_PAIR:(p + 1) * KT_ROWS, :] = ones_rows

    @pl.when(i == NPT)
    def _():
        qkvs_ref[:, :D_C] = u[:, S_Q:]
        qkvs_ref[:, D_C:2 * D_C] = _dot(hn, w_ref[:, S_K:S_V])
        qkvs_ref[:, 2 * D_C:] = v
        for t in range(DEC_SEQ):
            kts_ref[t] = kvt[:D_C, t * DEC_BATCH:(t + 1) * DEC_BATCH]
            vts_ref[t] = kvt[D_C:, t * DEC_BATCH:(t + 1) * DEC_BATCH]


def _in_proj(hs, norm_mix, w_in_b, w_kvt_b, kt_all, vt_all, layer):
    n_h = len(hs)
    tiles_per_seq = SEQ // TM
    pi = lambda i: jnp.minimum(i, NPT - 1)
    row = lambda w: pl.BlockSpec((TM, w), lambda i: (i, 0))
    prow = lambda w: pl.BlockSpec((TM, w), lambda i: (pi(i), 0))
    full = lambda *s: pl.BlockSpec(s, lambda i: (0,) * len(s))
    kv_t = pl.BlockSpec((None, None, D_C, TM), lambda i: (layer, pi(i) // tiles_per_seq, 0, pi(i) % tiles_per_seq))
    ktb = pl.BlockSpec((None, N_PAIRS * KT_ROWS, TM), lambda i: (pi(i) // tiles_per_seq, 0, pi(i) % tiles_per_seq))
    any_spec = pl.BlockSpec(memory_space=pl.ANY)
    kv_shape = jax.ShapeDtypeStruct((DEPTH, BATCH, D_C, SEQ), F32)
    return pl.pallas_call(
        functools.partial(_in_proj_kernel, n_h),
        grid=(NTT,),
        in_specs=_h_specs(n_h == 2) + [_layer_spec((1, D_MODEL), layer), _layer_spec((D_MODEL, D_IN), layer),
                                       _layer_spec((2 * D_C, D_MODEL), layer), any_spec, any_spec],
        out_specs=[row(2 * D_A), row(D_B), prow(D_C), prow(D_C), ktb, kv_t, kv_t,
                   full(NS, 3 * D_C), full(DEC_SEQ, D_C, DEC_BATCH), full(DEC_SEQ, D_C, DEC_BATCH)],
        out_shape=[jax.ShapeDtypeStruct((NT, 2 * D_A), F32), jax.ShapeDtypeStruct((NT, D_B), F32),
                   jax.ShapeDtypeStruct((NP, D_C), BF16), jax.ShapeDtypeStruct((NP, D_C), BF16),
                   jax.ShapeDtypeStruct((BATCH, N_PAIRS * KT_ROWS, SEQ), BF16), kv_shape, kv_shape,
                   jax.ShapeDtypeStruct((NS, 3 * D_C), F32),
                   jax.ShapeDtypeStruct((DEC_SEQ, D_C, DEC_BATCH), F32),
                   jax.ShapeDtypeStruct((DEC_SEQ, D_C, DEC_BATCH), F32)],
        input_output_aliases={n_h + 3: 5, n_h + 4: 6},
        compiler_params=_params("arbitrary"),
        name="in_proj",
    )(*hs, norm_mix, w_in_b, w_kvt_b, kt_all, vt_all)


def _rg_gates(xc, wa, ba, wx, bx, lam):
    xcb = xc.astype(BF16)
    r = jax.nn.sigmoid(_dot(xcb, wa) + ba)
    ig = jax.nn.sigmoid(_dot(xcb, wx) + bx)
    log_a = RG_C * r * jax.nn.log_sigmoid(lam)
    a = jnp.exp(log_a)
    om = 1.0 - a * a
    u = jnp.where(om > 0.0, om * lax.rsqrt(om), 0.0) * (ig * xc)
    return a, u


def _pool_lane_select(lane, per_window):
    out = per_window[-1]
    for g in range(len(POOL_WINDOWS) - 2, -1, -1):
        out = jnp.where(lane < (g + 1) * POOL_GROUP, per_window[g], out)
    return out


def _mix_prompt_kernel(xag_ref, xb_ref, cw_ref, cb_ref, wa_ref, ba_ref, wx_ref, bx_ref, lam_ref,
                       na_ref, pw_ref, ps_ref,
                       ya_ref, yb_ref, convn_ref, rgn_ref, pooln_ref,
                       xbuf, pbuf, hcar, hs_s):
    j = pl.program_id(1)
    hist_a = 8
    hist_b = 16

    @pl.when(j == 0)
    def _():
        xbuf[0:hist_a, :] = jnp.zeros((hist_a, D_A), F32)
        pbuf[0:hist_b, :] = jnp.zeros((hist_b, D_B), F32)
        hcar[...] = jnp.zeros_like(hcar)

    xa = xag_ref[:, 0:D_A]
    ga = xag_ref[:, D_A:2 * D_A]
    xbuf[hist_a:hist_a + TT, :] = xa
    cw = cw_ref[...]
    assert CONV_WIDTH == 4
    e = xbuf[...]
    e1 = pltpu.roll(e, 1, 0)
    p2 = pltpu.roll(cw[1:2, :] * e + cw[0:1, :] * e1, 2, 0)
    xc = (cb_ref[...] + cw[3:4, :] * e + cw[2:3, :] * e1 + p2)[hist_a:hist_a + TT]
    convn_ref[0] = xbuf[hist_a + TT - (CONV_WIDTH - 1):hist_a + TT, :]
    xbuf[0:hist_a, :] = xbuf[TT:TT + hist_a, :]

    a, u = _rg_gates(xc, wa_ref[...], ba_ref[...], wx_ref[...], bx_ref[...], lam_ref[...])
    sub_rows = 8
    a3 = a.reshape(TT // sub_rows, sub_rows, D_A)
    u3 = u.reshape(TT // sub_rows, sub_rows, D_A)
    sub = lax.broadcasted_iota(jnp.int32, a3.shape, 1)
    d = 1
    while d < sub_rows:
        m = sub >= d
        a_s = jnp.where(m, pltpu.roll(a3, d, 1), 1.0)
        u_s = jnp.where(m, pltpu.roll(u3, d, 1), 0.0)
        u3 = a3 * u_s + u3
        a3 = a3 * a_s
        d *= 2
    h = hcar[0:1, :]
    for g in range(TT // sub_rows):
        hs_g = u3[g] + a3[g] * h
        hs_s[g * sub_rows:(g + 1) * sub_rows, :] = hs_g
        h = hs_g[sub_rows - 1:sub_rows, :]
    hcar[0:1, :] = h
    rgn_ref[0] = h
    hs = hs_s[...]
    ya_ref[...] = _rms(jax.nn.gelu(ga) * hs, na_ref[...])

    xb = xb_ref[...]
    pbuf[hist_b:hist_b + TT, :] = xb
    lane = lax.broadcasted_iota(jnp.int32, (TT, D_B), 1)
    pos = j * TT + lax.broadcasted_iota(jnp.int32, (TT, D_B), 0)
    assert POOL_WINDOWS == (2, 4, 8, 16) and D_B == 2 * LANE and 2 * POOL_GROUP == LANE
    e = pbuf[...]
    s2 = e + pltpu.roll(e, 1, 0)
    s4 = s2 + pltpu.roll(s2, 2, 0)
    s4_hi = s4[:, LANE:]
    s8_hi = s4_hi + pltpu.roll(s4_hi, 4, 0)
    s16_hi = s8_hi + pltpu.roll(s8_hi, 8, 0)
    tile = lambda x: x[hist_b:hist_b + TT]
    lane_h = lax.broadcasted_iota(jnp.int32, (TT, LANE), 1)
    wsum = jnp.concatenate(
        [jnp.where(lane_h < POOL_GROUP, tile(s2[:, :LANE]), tile(s4[:, :LANE])),
         jnp.where(lane_h < POOL_GROUP, tile(s8_hi), tile(s16_hi))], axis=1)
    win = _pool_lane_select(lane, [jnp.full((TT, D_B), w, jnp.int32) for w in POOL_WINDOWS])
    cnt = jnp.minimum(win, pos + 1).astype(F32)
    pooled = (wsum / cnt - xb).astype(BF16)
    yb_ref[...] = _dot(pooled, pw_ref[...]) * ps_ref[...]
    pooln_ref[0] = pbuf[hist_b + TT - POOL_BUF:hist_b + TT, :]
    pbuf[0:hist_b, :] = pbuf[TT:TT + hist_b, :]


def _mix_prompt(xag, xb, lw, layer):
    nj = SEQ // TT
    row = lambda w: pl.BlockSpec((TT, w), lambda b, j: (b * nj + j, 0))
    vec = lambda w: _layer_spec((1, w), layer)
    state = lambda r, w: pl.BlockSpec((1, r, w), lambda b, j: (b, 0, 0))
    return pl.pallas_call(
        _mix_prompt_kernel,
        grid=(BATCH, nj),
        in_specs=[row(2 * D_A), row(D_B), _layer_spec((CONV_WIDTH, D_A), layer), vec(D_A),
                  _layer_spec((D_A, D_A), layer), vec(D_A), _layer_spec((D_A, D_A), layer), vec(D_A),
                  vec(D_A), vec(D_A), _layer_spec((D_B, D_B), layer), vec(D_B)],
        out_specs=[row(D_A), row(D_B), state(CONV_WIDTH - 1, D_A), state(1, D_A), state(POOL_BUF, D_B)],
        out_shape=[jax.ShapeDtypeStruct((NP, D_A), F32), jax.ShapeDtypeStruct((NP, D_B), F32),
                   jax.ShapeDtypeStruct((BATCH, CONV_WIDTH - 1, D_A), F32),
                   jax.ShapeDtypeStruct((BATCH, 1, D_A), F32),
                   jax.ShapeDtypeStruct((BATCH, POOL_BUF, D_B), F32)],
        scratch_shapes=[pltpu.VMEM((TT + 8, D_A), F32), pltpu.VMEM((TT + 16, D_B), F32),
                        pltpu.VMEM((8, D_A), F32), pltpu.VMEM((TT, D_A), F32)],
        compiler_params=_params("parallel", "arbitrary"),
        name="mix_prompt",
    )(xag, xb, lw["conv_w"], lw["conv_b"], lw["rg_wa"], lw["rg_ba"], lw["rg_wx"], lw["rg_bx"],
      lw["rg_lambda"], lw["norm_a"], lw["pool_w"], lw["pool_scale"])


def _mix_sample_kernel(xag_ref, xb_ref, sconv_ref, srg_ref, spool_ref,
                       cw_ref, cb_ref, wa_ref, ba_ref, wx_ref, bx_ref, lam_ref, na_ref, pw_ref, ps_ref,
                       ya_ref, yb_ref, convn_ref, rgn_ref, pooln_ref):
    nb = DEC_BATCH
    rows = lambda t: slice(t * nb, (t + 1) * nb)
    xe = [sconv_ref[0, t] for t in range(CONV_WIDTH - 1)] + [xag_ref[rows(t), 0:D_A] for t in range(DEC_SEQ)]
    cw = cw_ref[...]
    xcs = []
    for t in range(DEC_SEQ):
        y = cb_ref[...]
        for jj in range(CONV_WIDTH):
            y = y + cw[jj:jj + 1, :] * xe[t + jj]
        xcs.append(y)
    for t in range(CONV_WIDTH - 1):
        convn_ref[t] = xe[len(xe) - (CONV_WIDTH - 1) + t]
    xc = jnp.concatenate(xcs, axis=0)
    a, u = _rg_gates(xc, wa_ref[...], ba_ref[...], wx_ref[...], bx_ref[...], lam_ref[...])
    h = srg_ref[0]
    hs = []
    for t in range(DEC_SEQ):
        h = a[rows(t)] * h + u[rows(t)]
        hs.append(h)
    rgn_ref[...] = h
    ga = xag_ref[:, D_A:2 * D_A]
    ya_ref[...] = _rms(jax.nn.gelu(ga) * jnp.concatenate(hs, axis=0), na_ref[...])

    pe = [spool_ref[0, t] for t in range(POOL_BUF)] + [xb_ref[rows(t), :] for t in range(DEC_SEQ)]
    for t in range(POOL_BUF):
        pooln_ref[t] = pe[len(pe) - POOL_BUF + t]
    lane = lax.broadcasted_iota(jnp.int32, (nb, D_B), 1)
    pooled = []
    for t in range(DEC_SEQ):
        s = pe[POOL_BUF + t]
        means = []
        for lag in range(1, max(POOL_WINDOWS)):
            s = s + pe[POOL_BUF + t - lag]
            if lag + 1 in POOL_WINDOWS:
                means.append(s / float(min(lag + 1, PAST_LEN + t + 1)))
        pooled.append(_pool_lane_select(lane, means) - pe[POOL_BUF + t])
    pooled = jnp.concatenate(pooled, axis=0).astype(BF16)
    yb_ref[...] = _dot(pooled, pw_ref[...]) * ps_ref[...]


def _mix_sample(xag, xb, sconv_t, srg, spool_t, lw, layer):
    vec = lambda w: _layer_spec((1, w), layer)
    lay = lambda *s: pl.BlockSpec((1,) + s, lambda i: (layer,) + (0,) * len(s))
    full = lambda *s: pl.BlockSpec(s, lambda i: (0,) * len(s))
    return pl.pallas_call(
        _mix_sample_kernel,
        grid=(1,),
        in_specs=[pl.BlockSpec((NS, 2 * D_A), lambda i: (NPT, 0)), pl.BlockSpec((NS, D_B), lambda i: (NPT, 0)),
                  lay(CONV_WIDTH - 1, DEC_BATCH, D_A), lay(DEC_BATCH, D_A), lay(POOL_BUF, DEC_BATCH, D_B),
                  _layer_spec((CONV_WIDTH, D_A), layer), vec(D_A),
                  _layer_spec((D_A, D_A), layer), vec(D_A), _layer_spec((D_A, D_A), layer), vec(D_A),
                  vec(D_A), vec(D_A), _layer_spec((D_B, D_B), layer), vec(D_B)],
        out_specs=[full(NS, D_A), full(NS, D_B), full(CONV_WIDTH - 1, DEC_BATCH, D_A), full(DEC_BATCH, D_A),
                   full(POOL_BUF, DEC_BATCH, D_B)],
        out_shape=[jax.ShapeDtypeStruct((NS, D_A), F32), jax.ShapeDtypeStruct((NS, D_B), F32),
                   jax.ShapeDtypeStruct((CONV_WIDTH - 1, DEC_BATCH, D_A), F32),
                   jax.ShapeDtypeStruct((DEC_BATCH, D_A), F32),
                   jax.ShapeDtypeStruct((POOL_BUF, DEC_BATCH, D_B), F32)],
        compiler_params=_params("arbitrary"),
        name="mix_sample",
    )(xag, xb, sconv_t, srg, spool_t, lw["conv_w"], lw["conv_b"], lw["rg_wa"], lw["rg_ba"], lw["rg_wx"],
      lw["rg_bx"], lw["rg_lambda"], lw["norm_a"], lw["pool_w"], lw["pool_scale"])


def _attn_prompt_kernel(bias_ref, q_ref, kt_ref, v_ref, o_ref, za_s, zb_s, zc_s, zd_s, acc_s, run_s):
    assert TQ == 2 * TK
    p = pl.program_id(1)
    i = pl.program_id(2)
    q = q_ref[...]
    lane_q = lax.broadcasted_iota(jnp.int32, (TQ, HEAD_PAIR), 1)
    qs = jnp.concatenate([jnp.where(lane_q < HEAD_DIM, q, jnp.zeros_like(q)),
                          jnp.where(lane_q >= HEAD_DIM, q, jnp.zeros_like(q))], axis=0) * jnp.asarray(SCALE, BF16)

    def offset_lanes(bias):
        b = jnp.full((TQ, HEAD_PAIR), bias, F32)
        terms = []
        for _ in range(BIAS_ROWS):
            terms.append(b.astype(BF16).astype(F32))
            b = b - terms[-1]
        out = jnp.zeros((TQ, HEAD_PAIR), F32)
        for n in range(BIAS_ROWS - 1, -1, -1):
            out = jnp.where(lane_q == n, terms[n], out)
        return out.astype(BF16)

    qs = jnp.concatenate([qs, jnp.concatenate([offset_lanes(bias_ref[2 * p]), offset_lanes(bias_ref[2 * p + 1])],
                                              axis=0)], axis=1)
    r = lax.broadcasted_iota(jnp.int32, (2 * TQ, TK), 0) & (TQ - 1)
    c = lax.broadcasted_iota(jnp.int32, (2 * TQ, TK), 1)
    ones = _suffix_ones(TK)

    def scores(j):
        start = pl.multiple_of(j * TK, TK)
        return _dot(qs, kt_ref[:, pl.ds(start, TK)])

    def process(z_ref, j, diag_off):
        z = z_ref[...]
        start = pl.multiple_of(j * TK, TK)
        sp = _softplus(z)
        if diag_off is not None:
            causal = c + diag_off * TK < r
            sp = jnp.where(causal, sp, 0.0)
        run = run_s[...]
        cs = _dot(_hi_lo(sp), ones)
        a = jnp.exp(z - (cs + jnp.concatenate([run] * (TK // LANE), axis=1)))
        if diag_off is not None:
            a = jnp.where(causal, a, 0.0)
        acc_s[...] += _dot(a, v_ref[pl.ds(start, TK), :])
        run_s[...] = run + jnp.broadcast_to(cs[:, 0:1], run.shape)

    def process_last_tile(j):
        lower = lambda x: jnp.concatenate([x[TK:TQ], x[TQ + TK:]], axis=0)
        start = pl.multiple_of(j * TK, TK)
        z = _dot(lower(qs), kt_ref[:, pl.ds(start, TK)])
        rows = lax.broadcasted_iota(jnp.int32, (TQ, TK), 0) & (TK - 1)
        causal = lax.broadcasted_iota(jnp.int32, (TQ, TK), 1) < rows
        sp = jnp.where(causal, _softplus(z), 0.0)
        cs = _dot(_hi_lo(sp), ones)
        a = jnp.where(causal, jnp.exp(z - cs), 0.0)
        av = _dot(a, v_ref[pl.ds(start, TK), :])
        tot = jnp.broadcast_to(cs[:, 0:1], (TQ, HEAD_PAIR))
        zeros = jnp.zeros((TK, HEAD_PAIR), F32)
        acc_s[...] = jnp.concatenate([zeros, av[:TK], zeros, av[TK:]], axis=0)
        run_s[...] = jnp.concatenate([zeros, tot[:TK], zeros, tot[TK:]], axis=0)

    t0 = 2 * i + 1
    zb_s[...] = scores(t0 - 1)
    process_last_tile(t0)
    za_s[...] = scores(jnp.maximum(t0 - 2, 0))
    process(zb_s, t0 - 1, 0)

    n_four = i // 2

    def four_tiles(jj, carry):
        t = t0 - 2 - 4 * jj
        zb_s[...] = scores(t - 1)
        process(za_s, t, None)
        zc_s[...] = scores(t - 2)
        process(zb_s, t - 1, None)
        zd_s[...] = scores(t - 3)
        process(zc_s, t - 2, None)
        za_s[...] = scores(jnp.maximum(t - 4, 0))
        process(zd_s, t - 3, None)
        return carry

    def two_tiles(jj, carry):
        t = t0 - 2 - 4 * n_four
        zb_s[...] = scores(t - 1)
        process(za_s, t, None)
        process(zb_s, t - 1, None)
        return carry

    lax.fori_loop(0, n_four, four_tiles, 0)
    lax.fori_loop(0, i - 2 * n_four, two_tiles, 0)
    acc = acc_s[...]
    o_ref[...] = jnp.where(lane_q < HEAD_DIM, acc[:TQ], acc[TQ:])


def _attn_prompt(bias, qb, ktb, vb):
    nq = SEQ // TQ
    return pl.pallas_call(
        _attn_prompt_kernel,
        grid=(BATCH, N_PAIRS, nq),
        in_specs=[pl.BlockSpec(memory_space=pltpu.SMEM),
                  pl.BlockSpec((TQ, HEAD_PAIR), lambda b, p, i: (b * nq + i, p)),
                  pl.BlockSpec((None, KT_ROWS, SEQ), lambda b, p, i: (b, p, 0)),
                  pl.BlockSpec((SEQ, HEAD_PAIR), lambda b, p, i: (b, p))],
        out_specs=pl.BlockSpec((TQ, HEAD_PAIR), lambda b, p, i: (b * nq + i, p)),
        out_shape=jax.ShapeDtypeStruct((NP, D_C), F32),
        scratch_shapes=[pltpu.VMEM((2 * TQ, TK), F32)] * 4
        + [pltpu.VMEM((2 * TQ, HEAD_PAIR), F32), pltpu.VMEM((2 * TQ, HEAD_PAIR), F32)],
        compiler_params=_params("parallel", "parallel", "arbitrary"),
        name="attn_prompt",
    )(bias, qb, ktb, vb)


QROWS = 32
N_BLOCKS = N_PAGES + 1


def _attn_decode_kernel(layer, pt_ref, bias_ref, q8_ref, kn_ref, vn_ref, ckt_hbm, cvt_hbm, o_ref,
                        kbuf, vbuf, kn_s, vn_s, sem):
    b = pl.program_id(0)
    slot = b & 1

    def page_copies(seq, slot):
        cps = []
        for pg in range(N_PAGES):
            page = pt_ref[seq * N_PAGES + pg]
            lanes = pl.ds(pg * PAGE_SIZE, PAGE_SIZE)
            cps.append(pltpu.make_async_copy(ckt_hbm.at[layer, page], kbuf.at[slot, :, lanes], sem.at[slot]))
            cps.append(pltpu.make_async_copy(cvt_hbm.at[layer, page], vbuf.at[slot, :, lanes], sem.at[slot]))
        return cps

    @pl.when(b == 0)
    def _():
        kn_s[...] = jnp.zeros_like(kn_s)
        vn_s[...] = jnp.zeros_like(vn_s)
        for cp in page_copies(0, 0):
            cp.start()

    @pl.when(b + 1 < DEC_BATCH)
    def _():
        for cp in page_copies(b + 1, 1 - slot):
            cp.start()

    kn_s[0:DEC_SEQ, :] = kn_ref[0]
    vn_s[0:DEC_SEQ, :] = vn_ref[0]

    rq = lax.broadcasted_iota(jnp.int32, (QROWS, D_C), 0)
    cq = lax.broadcasted_iota(jnp.int32, (QROWS, D_C), 1)
    head_lanes = (cq >> 6) == (rq >> 2)
    q8 = q8_ref[0] * SCALE
    qbd = jnp.where(head_lanes, jnp.concatenate([q8] * (QROWS // 8), axis=0), 0.0)

    rr = lax.broadcasted_iota(jnp.int32, (QROWS, PAGE_SIZE), 0)
    cc = lax.broadcasted_iota(jnp.int32, (QROWS, PAGE_SIZE), 1)
    bias = jnp.zeros((QROWS, PAGE_SIZE), F32)
    for h in range(N_C_HEADS):
        bias = jnp.where((rr >> 2) == h, bias_ref[h], bias)
    new_valid = cc < (rr & 3)
    z_new = _dot_nt(qbd, kn_s[...]) + bias

    for cp in page_copies(b, slot):
        cp.wait()

    z_wide = _dot(qbd, kbuf[slot])
    z_old = jnp.concatenate([z_wide[:, n * PAGE_SIZE:(n + 1) * PAGE_SIZE] + bias for n in range(N_PAGES)], axis=0)
    sp = jnp.concatenate([_softplus(z_old), jnp.where(new_valid, _softplus(z_new), 0.0)], axis=0)
    z = jnp.concatenate([z_old, z_new], axis=0)
    cs = _dot(_hi_lo(sp), _suffix_ones(PAGE_SIZE))

    run = jnp.zeros((QROWS, 1), F32)
    runs = [None] * N_BLOCKS
    for n in range(N_BLOCKS - 1, -1, -1):
        runs[n] = jnp.broadcast_to(run, (QROWS, PAGE_SIZE))
        run = run + cs[n * QROWS:(n + 1) * QROWS, 0:1]
    a = jnp.exp(z - (cs + jnp.concatenate(runs, axis=0)))
    blk = lambda n: a[n * QROWS:(n + 1) * QROWS]

    a_wide = jnp.concatenate([blk(n) for n in range(N_PAGES)], axis=1)
    acc = _dot(jnp.where(new_valid, blk(N_PAGES), 0.0), vn_s[...]) + _dot_nt(a_wide, vbuf[slot])

    acc = jnp.where(head_lanes, acc, 0.0)
    s = acc[0:8] + acc[8:16] + acc[16:24]
    s = s + pltpu.roll(s, 4, 0)
    o_ref[0] = s[0:DEC_SEQ]


def _attn_decode(page_table_flat, bias, q8, kn, vn, cache_kt, cache_vt, layer):
    seq = lambda r: pl.BlockSpec((1, r, D_C), lambda b, pt: (b, 0, 0))
    hbm = pl.BlockSpec(memory_space=pl.ANY)
    grid_spec = pltpu.PrefetchScalarGridSpec(
        num_scalar_prefetch=1,
        grid=(DEC_BATCH,),
        in_specs=[pl.BlockSpec(memory_space=pltpu.SMEM), seq(8), seq(DEC_SEQ), seq(DEC_SEQ), hbm, hbm],
        out_specs=seq(DEC_SEQ),
        scratch_shapes=[pltpu.VMEM((2, D_C, PAST_LEN), F32), pltpu.VMEM((2, D_C, PAST_LEN), F32),
                        pltpu.VMEM((PAGE_SIZE, D_C), F32), pltpu.VMEM((PAGE_SIZE, D_C), F32),
                        pltpu.SemaphoreType.DMA((2,))],
    )
    return pl.pallas_call(
        functools.partial(_attn_decode_kernel, layer),
        grid_spec=grid_spec,
        out_shape=jax.ShapeDtypeStruct((DEC_BATCH, DEC_SEQ, D_C), F32),
        compiler_params=_params("arbitrary"),
        name="attn_decode",
    )(page_table_flat, bias, q8, kn, vn, cache_kt, cache_vt)


MXU_TILE = 256
FF_SPLIT = (D_FF // MXU_TILE + 1) // 2 * MXU_TILE


def _tail_kernel(last, n_h, yap_ref, ybp_ref, op_ref, pep_ref, yas_ref, ybs_ref, os_ref, pes_ref, *refs):
    h_refs = refs[:n_h]
    nc_ref, wout_ref, nf_ref, wg_ref, wu_ref, wd_ref, np_ref, wpg_ref, wpp_ref, nfin_ref = refs[n_h:n_h + 10]
    out_refs = refs[n_h + 10:]
    i = pl.program_id(0)
    is_sample = i == NPT
    pick = lambda s_ref, p_ref: jnp.where(is_sample, s_ref[...], p_ref[...])
    yc = _rms(pick(os_ref, op_ref), nc_ref[...])
    y = jnp.concatenate([pick(yas_ref, yap_ref), pick(ybs_ref, ybp_ref), yc], axis=-1).astype(BF16)
    h = _load_h(h_refs) + _dot(y, wout_ref[...])

    hn = _rms(h, nf_ref[...]).astype(BF16)
    for cols in (slice(0, FF_SPLIT), slice(FF_SPLIT, D_FF)):
        act = (jax.nn.silu(_dot(hn, wg_ref[:, cols])) * _dot(hn, wu_ref[:, cols])).astype(BF16)
        h = h + _dot(act, wd_ref[cols, :])

    hn = _rms(h, np_ref[...]).astype(BF16)
    gate = jax.nn.sigmoid(_dot(hn, wpg_ref[...]))
    h = h + gate * _dot(pick(pes_ref, pep_ref).astype(BF16), wpp_ref[...])

    if not last:
        out_refs[0][...] = h
    else:
        y_out = _rms(h, nfin_ref[...])
        yp_ref, ys_ref = out_refs

        @pl.when(i < NPT)
        def _():
            yp_ref[...] = y_out

        @pl.when(is_sample)
        def _():
            ys_ref[...] = y_out


def _tail(ya_p, yb_p, o_p, pe_p, ya_s, yb_s, o_s, pe_s, hs, w, norm_final, layer):
    last = layer == DEPTH - 1
    n_h = len(hs)
    pi = lambda i: jnp.minimum(i, NPT - 1)
    prow = lambda wd: pl.BlockSpec((TM, wd), lambda i: (pi(i), 0))
    srow = lambda wd: pl.BlockSpec((NS, wd), lambda i: (0, 0), pipeline_mode=pl.Buffered(1))
    row = lambda wd: pl.BlockSpec((TM, wd), lambda i: (i, 0))
    vec = lambda wd: _layer_spec((1, wd), layer)
    if last:
        out_specs = [prow(D_MODEL), pl.BlockSpec((NS, D_MODEL), lambda i: (0, 0))]
        out_shape = [jax.ShapeDtypeStruct((NP, D_MODEL), F32), jax.ShapeDtypeStruct((NS, D_MODEL), F32)]
    else:
        out_specs = [row(D_MODEL)]
        out_shape = [jax.ShapeDtypeStruct((NT, D_MODEL), F32)]
    return pl.pallas_call(
        functools.partial(_tail_kernel, last, n_h),
        grid=(NTT,),
        in_specs=[prow(D_A), prow(D_B), prow(D_C),
                  pl.BlockSpec((None, TM, D_PLE), lambda i: (layer, pi(i), 0)),
                  srow(D_A), srow(D_B), srow(D_C),
                  pl.BlockSpec((None, NS, D_PLE), lambda i: (layer, 0, 0), pipeline_mode=pl.Buffered(1))]
        + _h_specs(n_h == 2)
        + [vec(D_C), _layer_spec((D_MODEL, D_MODEL), layer),
                  vec(D_MODEL), _layer_spec((D_MODEL, D_FF), layer), _layer_spec((D_MODEL, D_FF), layer),
                  _layer_spec((D_FF, D_MODEL), layer),
                  vec(D_MODEL), _layer_spec((D_MODEL, D_MODEL), layer), _layer_spec((D_PLE, D_MODEL), layer),
                  pl.BlockSpec((1, D_MODEL), lambda i: (0, 0), pipeline_mode=pl.Buffered(1))],
        out_specs=out_specs,
        out_shape=out_shape,
        compiler_params=_params("arbitrary"),
        name="tail",
    )(ya_p, yb_p, o_p, pe_p, ya_s, yb_s, o_s, pe_s, *hs,
      w["norm_c"], w["w_out"], w["norm_ffn"], w["w_ff_gate"], w["w_ff_up"], w["w_ff_down"],
      w["norm_ple"], w["w_ple_gate"], w["w_ple_proj"], norm_final)


def _block_diag(w):
    depth, g, n, _ = w.shape
    eye = jnp.eye(g, dtype=w.dtype)
    return jnp.einsum("dgij,gh->dgihj", w, eye).reshape(depth, g * n, g * n)


def kernel(x_prompt, x_sample, cache_k, cache_v, state_conv, state_rglru, state_pool, page_table,
           p_prompt, p_sample, norm_mix, w_in, conv_w, conv_b, rg_wa, rg_ba, rg_wx, rg_bx, rg_lambda,
           norm_a, pool_w, pool_scale, norm_c, sb_bias, w_out, norm_ffn, w_ff_gate, w_ff_up, w_ff_down,
           norm_ple, w_ple_gate, w_ple_proj, norm_final):
    n_pool = cache_k.shape[1]
    tmajor = lambda x: jnp.swapaxes(x, -3, -2)

    hs = (x_prompt.reshape(NP, D_MODEL), tmajor(x_sample).reshape(NS, D_MODEL))
    pe_p = p_prompt.reshape(DEPTH, NP, D_PLE)
    pe_s = tmajor(p_sample).reshape(DEPTH, NS, D_PLE)
    ckt = jnp.transpose(cache_k, (0, 1, 3, 4, 2)).reshape(DEPTH, n_pool, D_C, PAGE_SIZE)
    cvt = jnp.transpose(cache_v, (0, 1, 3, 4, 2)).reshape(DEPTH, n_pool, D_C, PAGE_SIZE)
    pt = page_table.reshape(-1)
    sconv_t = tmajor(state_conv)
    spool_t = tmajor(state_pool)

    vec = lambda x: x.reshape(DEPTH, 1, -1)
    lw = {
        "conv_w": conv_w, "conv_b": vec(conv_b),
        "rg_wa": _block_diag(rg_wa).astype(BF16), "rg_ba": vec(rg_ba),
        "rg_wx": _block_diag(rg_wx).astype(BF16), "rg_bx": vec(rg_bx),
        "rg_lambda": vec(rg_lambda), "norm_a": vec(norm_a),
        "pool_w": _block_diag(pool_w).astype(BF16), "pool_scale": vec(pool_scale),
    }
    w_in_b = w_in.astype(BF16)
    w_kvt_b = jnp.swapaxes(lax.optimization_barrier(w_in[:, :, S_K:]), 1, 2).astype(BF16)
    norm_mix3 = vec(norm_mix)
    tw = {
        "norm_c": vec(norm_c), "w_out": w_out.astype(BF16), "norm_ffn": vec(norm_ffn),
        "w_ff_gate": w_ff_gate.astype(BF16), "w_ff_up": w_ff_up.astype(BF16), "w_ff_down": w_ff_down.astype(BF16),
        "norm_ple": vec(norm_ple), "w_ple_gate": w_ple_gate.astype(BF16), "w_ple_proj": w_ple_proj.astype(BF16),
    }
    norm_final2 = norm_final.reshape(1, D_MODEL)

    kt_all = jnp.zeros((DEPTH, BATCH, D_C, SEQ), F32)
    vt_all = jnp.zeros((DEPTH, BATCH, D_C, SEQ), F32)
    outs = {n: [] for n in ("ks", "vs", "cp", "cs", "rp", "rs", "pp", "ps")}
    for l in range(DEPTH):
        xag, xb, qb, vb, ktb, kt_all, vt_all, qkv_s, kt_s, vt_s = _in_proj(
            hs, norm_mix3, w_in_b, w_kvt_b, kt_all, vt_all, l)
        ya_p, yb_p, conv_p, rg_p, pool_p = _mix_prompt(xag, xb, lw, l)
        ya_s, yb_s, conv_s, rg_s, pool_s = _mix_sample(xag, xb, sconv_t, state_rglru, spool_t, lw, l)
        o_p = _attn_prompt(sb_bias[l], qb, ktb, vb)

        bmajor = lambda x: tmajor(x.reshape(DEC_SEQ, DEC_BATCH, D_C))
        q_s = bmajor(qkv_s[:, :D_C])
        o_s = _attn_decode(pt, sb_bias[l], jnp.concatenate([q_s, q_s], axis=1),
                           bmajor(qkv_s[:, D_C:2 * D_C]), bmajor(qkv_s[:, 2 * D_C:]), ckt, cvt, l)
        o_s = tmajor(o_s).reshape(NS, D_C)

        res = _tail(ya_p, yb_p, o_p, pe_p, ya_s, yb_s, o_s, pe_s, hs, tw, norm_final2, l)
        hs = (res[0],)

        outs["ks"].append(kt_s)
        outs["vs"].append(vt_s)
        outs["cp"].append(conv_p)
        outs["cs"].append(conv_s)
        outs["rp"].append(rg_p.reshape(BATCH, D_A))
        outs["rs"].append(rg_s)
        outs["pp"].append(pool_p)
        outs["ps"].append(pool_s)

    y_prompt = res[0].reshape(BATCH, SEQ, D_MODEL)
    y_sample = tmajor(res[1].reshape(DEC_SEQ, DEC_BATCH, D_MODEL))
    st = lambda n: jnp.stack(outs[n])
    kv_prompt = lambda x: jnp.transpose(x.reshape(DEPTH, BATCH, N_C_HEADS, HEAD_DIM, SEQ), (0, 1, 4, 2, 3))
    kv_sample = lambda x: jnp.transpose(x.reshape(DEPTH, DEC_SEQ, N_C_HEADS, HEAD_DIM, DEC_BATCH), (0, 4, 1, 2, 3))
    return (y_prompt, y_sample, kv_prompt(kt_all), kv_prompt(vt_all), kv_sample(st("ks")), kv_sample(st("vs")),
            st("cp"), tmajor(st("cs")), st("rp"), st("rs"), st("pp"), tmajor(st("ps")))
```
